```python
import math
import jax, jax.numpy as jnp
from jax import lax
import numpy as np

D_MODEL = 2048
BATCH = 4
SEQ = 2048
DEPTH = 1
DEC_BATCH = 32
DEC_SEQ = 1
PAST_LEN = 16384
PAGE_SIZE = 128

HEAD_DIM = 128
MOBA_HEADS = 8
MOBA_KV_HEADS = 4
MOBA_BLOCK = 256
MOBA_TOPK = 3
MOBA_Q_CHUNK = 16
MOBA_WIDTH = MOBA_HEADS * HEAD_DIM
DIFF_HEADS = 4
DIFF_KV_HEADS = 2
DIFF_V_DIM = 2 * HEAD_DIM
DIFF_WIDTH = DIFF_HEADS * DIFF_V_DIM
DIFF_Q_BLOCK = 128
N_BRANCH = 2
PROJ_SIZES = (
    MOBA_HEADS * HEAD_DIM,
    MOBA_KV_HEADS * HEAD_DIM,
    MOBA_KV_HEADS * HEAD_DIM,
    MOBA_WIDTH,
    DIFF_HEADS * 2 * HEAD_DIM,
    DIFF_KV_HEADS * 2 * HEAD_DIM,
    DIFF_KV_HEADS * DIFF_V_DIM,
    DIFF_WIDTH,
    N_BRANCH * D_MODEL,
)
D_IN = sum(PROJ_SIZES)
V_COLUMN_GROUPS = (2, 6)
ALPHA = (2 * DEPTH) ** 0.25
BETA = (8 * DEPTH) ** -0.25
LN_EPS = 1e-5
RMS_EPS = 1e-5
NEG_INF = -1e30

kernel_name = 'moba_diffattn_gated_hybrid_step'


def alibi_slopes(n_heads):
    return jnp.asarray([2.0 ** (-8.0 * (i + 1) / n_heads) for i in range(n_heads)], dtype=jnp.float32)


def split_columns(h):
    cuts = []
    acc = 0
    for size in PROJ_SIZES[:-1]:
        acc += size
        cuts.append(acc)
    return jnp.split(h, cuts, axis=-1)


def layer_norm(x, g, b):
    xf = x.astype(jnp.float32)
    mu = jnp.mean(xf, axis=-1, keepdims=True)
    var = jnp.mean(jnp.square(xf - mu), axis=-1, keepdims=True)
    y = (xf - mu) * lax.rsqrt(var + LN_EPS) * g.astype(jnp.float32) + b.astype(jnp.float32)
    return y.astype(x.dtype)


def head_rms_norm(o, g):
    return o * lax.rsqrt(jnp.mean(jnp.square(o), axis=-1, keepdims=True) + RMS_EPS) * g.astype(jnp.float32)


def lambda_init(layer):
    return 0.8 - 0.6 * math.exp(-0.3 * layer)


def gather_past(cache, layer, page_table):
    rows = cache[layer, page_table]
    return rows.reshape((page_table.shape[0], page_table.shape[1] * PAGE_SIZE) + cache.shape[3:])


def pad_to_blocks(parts):
    total = sum(p.shape[1] for p in parts)
    pad = (-total) % MOBA_BLOCK
    parts = list(parts)
    if pad:
        p0 = parts[-1]
        parts.append(jnp.zeros((p0.shape[0], pad) + p0.shape[2:], p0.dtype))
    return parts[0] if len(parts) == 1 else jnp.concatenate(parts, axis=1)


def moba_attention(q, k, v, q_pos):
    B, Lq = q.shape[0], q.shape[1]
    n_blk = k.shape[1] // MOBA_BLOCK
    G = MOBA_HEADS // MOBA_KV_HEADS
    kb = k.reshape(B, n_blk, MOBA_BLOCK, MOBA_KV_HEADS, HEAD_DIM)
    vb = v.reshape(B, n_blk, MOBA_BLOCK, MOBA_KV_HEADS, HEAD_DIM)
    k_mean = jnp.mean(kb, axis=2, dtype=jnp.float32)
    n_sel = min(MOBA_TOPK, n_blk)
    qc = min(MOBA_Q_CHUNK, Lq)
    n_chunk = -(-Lq // qc)
    q_pad = n_chunk * qc - Lq
    qg = jnp.pad(q, ((0, 0), (0, q_pad), (0, 0), (0, 0)))
    qg = qg.reshape(B, n_chunk, qc, MOBA_KV_HEADS, G, HEAD_DIM).transpose(1, 0, 2, 3, 4, 5)
    pos = jnp.pad(q_pos, (0, q_pad), mode='edge').reshape(n_chunk, qc)
    slopes = alibi_slopes(MOBA_HEADS).reshape(MOBA_KV_HEADS, G)
    scale = HEAD_DIM ** -0.5
    blk_ids = jnp.arange(n_blk, dtype=jnp.int32)
    offs = jnp.arange(MOBA_BLOCK, dtype=jnp.int32)
    b_idx = jnp.arange(B)[:, None, None, None, None]
    h_idx = jnp.arange(MOBA_KV_HEADS)[None, None, :, None, None]

    def one_chunk(args):
        qq, pp = args
        q_blk = pp // MOBA_BLOCK
        gate = jnp.einsum('bqhgd,bnhd->bqhgn', qq.astype(jnp.float32), k_mean)
        fully_past = blk_ids[None, :] < q_blk[:, None]
        gate = jnp.where(fully_past[None, :, None, None, :], gate, NEG_INF)
        _, top = lax.top_k(gate, n_sel)
        own = jnp.broadcast_to(q_blk[None, :, None, None, None], top.shape[:-1] + (1,)).astype(top.dtype)
        sel = jnp.concatenate([top, own], axis=-1)
        sel_ok = jnp.concatenate([top < own, jnp.ones(own.shape, dtype=bool)], axis=-1)
        k_sel = kb[b_idx, sel, :, h_idx]
        v_sel = vb[b_idx, sel, :, h_idx]
        s = jnp.einsum('bqhgd,bqhgskd->bqhgsk', qq, k_sel).astype(jnp.float32) * scale
        k_pos = sel[..., None] * MOBA_BLOCK + offs
        dist = pp[None, :, None, None, None, None] - k_pos
        s = s - slopes[None, None, :, :, None, None] * dist.astype(jnp.float32)
        s = jnp.where(sel_ok[..., None] & (dist >= 0), s, NEG_INF)
        p = jax.nn.softmax(s.reshape(s.shape[:4] + (-1,)), axis=-1).reshape(s.shape)
        return jnp.einsum('bqhgsk,bqhgskd->bqhgd', p.astype(v_sel.dtype), v_sel)

    out = lax.map(one_chunk, (qg, pos))
    out = out.transpose(1, 0, 2, 3, 4, 5).reshape(B, n_chunk * qc, MOBA_HEADS, HEAD_DIM)
    return out[:, :Lq]


def diff_attention(q, k, v, q_pos, lam):
    B, Lq, _, G = q.shape[0], q.shape[1], q.shape[2], q.shape[3]
    T = k.shape[1]
    qb = min(DIFF_Q_BLOCK, Lq)
    n_chunk = -(-Lq // qb)
    q_pad = n_chunk * qb - Lq
    qg = jnp.pad(q, ((0, 0), (0, q_pad), (0, 0), (0, 0), (0, 0), (0, 0)))
    qg = qg.reshape(B, n_chunk, qb, DIFF_KV_HEADS, G, 2, HEAD_DIM).transpose(1, 0, 2, 3, 4, 5, 6)
    pos = jnp.pad(q_pos, (0, q_pad), mode='edge').reshape(n_chunk, qb)
    k2 = k.reshape(B, T, DIFF_KV_HEADS, 2, HEAD_DIM)
    k_pos = jnp.arange(T, dtype=jnp.int32)
    slopes = alibi_slopes(DIFF_HEADS).reshape(DIFF_KV_HEADS, G)
    scale = HEAD_DIM ** -0.5

    def one_block(args):
        qq, pp = args
        s = jnp.einsum('bqhgmd,bkhmd->bhgmqk', qq, k2).astype(jnp.float32) * scale
        dist = pp[:, None] - k_pos[None, :]
        s = s - slopes[None, :, :, None, None, None] * dist.astype(jnp.float32)
        s = jnp.where(dist >= 0, s, NEG_INF)
        p = jax.nn.softmax(s, axis=-1)
        w = p[:, :, :, 0] - lam * p[:, :, :, 1]
        return jnp.einsum('bhgqk,bkhd->bqhgd', w.astype(v.dtype), v)

    out = lax.map(one_block, (qg, pos))
    out = out.transpose(1, 0, 2, 3, 4, 5).reshape(B, n_chunk * qb, DIFF_HEADS, DIFF_V_DIM)
    return out[:, :Lq]


def hybrid_layer(x, q_pos, layer, paged, w_in, w_moba_o, w_diff_o, w_out,
                 lam_q1, lam_k1, lam_q2, lam_k2, diff_norm_g, ln_g, ln_b):
    B, L, _ = x.shape
    h = jnp.einsum('bld,de->ble', x, w_in)
    q_m, k_m, v_m, z_m, q_d, k_d, v_d, z_d, gates = split_columns(h)
    q_m = q_m.reshape(B, L, MOBA_HEADS, HEAD_DIM)
    k_m = k_m.reshape(B, L, MOBA_KV_HEADS, HEAD_DIM)
    v_m = v_m.reshape(B, L, MOBA_KV_HEADS, HEAD_DIM)
    q_d = q_d.reshape(B, L, DIFF_KV_HEADS, DIFF_HEADS // DIFF_KV_HEADS, 2, HEAD_DIM)
    k_d = k_d.reshape(B, L, DIFF_KV_HEADS, 2 * HEAD_DIM)
    v_d = v_d.reshape(B, L, DIFF_KV_HEADS, DIFF_V_DIM)

    def with_past(cache_idx, new, pad_blocks):
        if paged is None:
            parts = [new]
        else:
            parts = [gather_past(paged[cache_idx], layer, paged[4]), new]
        if pad_blocks:
            return pad_to_blocks(parts)
        return parts[0] if len(parts) == 1 else jnp.concatenate(parts, axis=1)

    o_m = moba_attention(q_m, with_past(0, k_m, True), with_past(1, v_m, True), q_pos)
    lam0 = lambda_init(layer)
    lam = (jnp.exp(jnp.sum(lam_q1.astype(jnp.float32) * lam_k1.astype(jnp.float32)))
           - jnp.exp(jnp.sum(lam_q2.astype(jnp.float32) * lam_k2.astype(jnp.float32))) + lam0)
    o_d = diff_attention(q_d, with_past(2, k_d, False), with_past(3, v_d, False), q_pos, lam)
    o_d = head_rms_norm(o_d.astype(jnp.float32), diff_norm_g) * (1.0 - lam0)

    y_m = jnp.einsum('blc,cd->bld', o_m.reshape(B, L, MOBA_WIDTH).astype(x.dtype) * jax.nn.silu(z_m), w_moba_o)
    y_d = jnp.einsum('blc,cd->bld', o_d.reshape(B, L, DIFF_WIDTH).astype(x.dtype) * jax.nn.silu(z_d), w_diff_o)
    g = jax.nn.sigmoid(gates.astype(jnp.float32)).astype(x.dtype).reshape(B, L, N_BRANCH, D_MODEL)
    merged = g[:, :, 0] * y_m + g[:, :, 1] * y_d
    f = jnp.einsum('bld,de->ble', merged, w_out)
    y = layer_norm(ALPHA * x + f, ln_g, ln_b)
    return y, (k_m, v_m, k_d, v_d)


def setup_inputs(seed: int = 0) -> dict:
    key = jax.random.key(seed)
    ks = jax.random.split(key, 20)
    n_pages = PAST_LEN // PAGE_SIZE
    n_used = DEC_BATCH * n_pages
    n_phys = n_used + max(1, n_used // 4)
    f32 = jnp.float32
    x_prompt = jax.random.normal(ks[0], (BATCH, SEQ, D_MODEL), f32)
    x_sample = jax.random.normal(ks[1], (DEC_BATCH, DEC_SEQ, D_MODEL), f32)
    cache_moba_k = jax.random.normal(ks[2], (DEPTH, n_phys, PAGE_SIZE, MOBA_KV_HEADS, HEAD_DIM), f32)
    cache_moba_v = jax.random.normal(ks[3], (DEPTH, n_phys, PAGE_SIZE, MOBA_KV_HEADS, HEAD_DIM), f32) * BETA
    cache_diff_k = jax.random.normal(ks[4], (DEPTH, n_phys, PAGE_SIZE, DIFF_KV_HEADS, 2 * HEAD_DIM), f32)
    cache_diff_v = jax.random.normal(ks[5], (DEPTH, n_phys, PAGE_SIZE, DIFF_KV_HEADS, DIFF_V_DIM), f32) * BETA
    page_table = jax.random.permutation(ks[6], n_phys)[:n_used].reshape(DEC_BATCH, n_pages).astype(jnp.int32)
    col_scale = jnp.concatenate([jnp.full((size,), BETA if i in V_COLUMN_GROUPS else 1.0, f32)
                                 for i, size in enumerate(PROJ_SIZES)])
    w_in = jax.random.normal(ks[7], (DEPTH, D_MODEL, D_IN), f32) * (D_MODEL ** -0.5) * col_scale
    w_moba_o = jax.random.normal(ks[8], (DEPTH, MOBA_WIDTH, D_MODEL), f32) * (MOBA_WIDTH ** -0.5) * BETA
    w_diff_o = jax.random.normal(ks[9], (DEPTH, DIFF_WIDTH, D_MODEL), f32) * (DIFF_WIDTH ** -0.5) * BETA
    w_out = jax.random.normal(ks[10], (DEPTH, D_MODEL, D_MODEL), f32) * (D_MODEL ** -0.5) * BETA
    lambda_q1 = jax.random.normal(ks[11], (DEPTH, HEAD_DIM), f32) * 0.1
    lambda_k1 = jax.random.normal(ks[12], (DEPTH, HEAD_DIM), f32) * 0.1
    lambda_q2 = jax.random.normal(ks[13], (DEPTH, HEAD_DIM), f32) * 0.1
    lambda_k2 = jax.random.normal(ks[14], (DEPTH, HEAD_DIM), f32) * 0.1
    diff_norm_g = 1.0 + 0.02 * jax.random.normal(ks[15], (DEPTH, DIFF_HEADS, DIFF_V_DIM), f32)
    ln_g = 1.0 + 0.02 * jax.random.normal(ks[16], (DEPTH, D_MODEL), f32)
    ln_b = 0.02 * jax.random.normal(ks[17], (DEPTH, D_MODEL), f32)
    return {'x_prompt': x_prompt, 'x_sample': x_sample,
            'cache_moba_k': cache_moba_k, 'cache_moba_v': cache_moba_v,
            'cache_diff_k': cache_diff_k, 'cache_diff_v': cache_diff_v,
            'page_table': page_table,
            'w_in': w_in, 'w_moba_o': w_moba_o, 'w_diff_o': w_diff_o, 'w_out': w_out,
            'lambda_q1': lambda_q1, 'lambda_k1': lambda_k1, 'lambda_q2': lambda_q2, 'lambda_k2': lambda_k2,
            'diff_norm_g': diff_norm_g, 'ln_g': ln_g, 'ln_b': ln_b}


def reference(x_prompt, x_sample, cache_moba_k, cache_moba_v, cache_diff_k, cache_diff_v, page_table,
              w_in, w_moba_o, w_diff_o, w_out, lambda_q1, lambda_k1, lambda_q2, lambda_k2,
              diff_norm_g, ln_g, ln_b):
    past_len = page_table.shape[1] * PAGE_SIZE
    pos_prompt = jnp.arange(x_prompt.shape[1], dtype=jnp.int32)
    pos_sample = past_len + jnp.arange(x_sample.shape[1], dtype=jnp.int32)
    paged = (cache_moba_k, cache_moba_v, cache_diff_k, cache_diff_v, page_table)
    h_p, h_s = x_prompt, x_sample
    rows_p, rows_s = [], []
    for layer in range(DEPTH):
        weights = (w_in[layer], w_moba_o[layer], w_diff_o[layer], w_out[layer],
                   lambda_q1[layer], lambda_k1[layer], lambda_q2[layer], lambda_k2[layer],
                   diff_norm_g[layer], ln_g[layer], ln_b[layer])
        h_p, kv_p = hybrid_layer(h_p, pos_prompt, layer, None, *weights)
        h_s, kv_s = hybrid_layer(h_s, pos_sample, layer, paged, *weights)
        rows_p.append(kv_p)
        rows_s.append(kv_s)
    moba_k_prompt = jnp.stack([r[0] for r in rows_p])
    moba_v_prompt = jnp.stack([r[1] for r in rows_p])
    diff_k_prompt = jnp.stack([r[2] for r in rows_p])
    diff_v_prompt = jnp.stack([r[3] for r in rows_p])
    moba_k_sample = jnp.stack([r[0] for r in rows_s])
    moba_v_sample = jnp.stack([r[1] for r in rows_s])
    diff_k_sample = jnp.stack([r[2] for r in rows_s])
    diff_v_sample = jnp.stack([r[3] for r in rows_s])
    return (h_p, h_s, moba_k_prompt, moba_v_prompt, diff_k_prompt, diff_v_prompt,
            moba_k_sample, moba_v_sample, diff_k_sample, diff_v_sample)
```

```python
import functools
import math

import jax
import jax.numpy as jnp
from jax import lax
from jax.experimental import pallas as pl
from jax.experimental.pallas import tpu as pltpu

F32 = jnp.float32
BF16 = jnp.bfloat16

D_MODEL = 2048
HEAD_DIM = 128
PAGE_SIZE = 128
MOBA_HEADS = 8
MOBA_KV_HEADS = 4
MOBA_BLOCK = 256
MOBA_TOPK = 3
DIFF_HEADS = 4
DIFF_KV_HEADS = 2
DIFF_V_DIM = 2 * HEAD_DIM
N_BRANCH = 2
DEPTH = 1
ALPHA = (2 * DEPTH) ** 0.25
LN_EPS = 1e-5
RMS_EPS = 1e-5
NEG_INF = -1e30
SCALE = HEAD_DIM ** -0.5
LAM0 = 0.8 - 0.6 * math.exp(-0.3 * 0)

VMEM_LIMIT_V7X = 56 * 1024 * 1024

PROJ_TN = 1024

_NT = (((1,), (1,)), ((), ()))


def _pow2_neg(k):
    return lax.bitcast_convert_type((127 - k) << 23, F32)


def _silu(v):
    return v * (1.0 / (1.0 + jnp.exp(-v)))


def _sigmoid(v):
    return 1.0 / (1.0 + jnp.exp(-v))


def _proj_kernel(x_ref, w_ref, qm_ref, km_ref, vm_ref, zm_ref, qd_ref, kd_ref, vd_ref, zd_ref, gt_ref, xb_ref):
    n = pl.program_id(1)

    @pl.when(n == 0)
    def _():
        xb_ref[...] = x_ref[...].astype(BF16)

    acc = jnp.dot(xb_ref[...], w_ref[...], preferred_element_type=F32)
    half = PROJ_TN // 2

    @pl.when(n == 0)
    def _():
        qm_ref[...] = acc.astype(BF16)

    @pl.when(n == 1)
    def _():
        km_ref[...] = acc[:, :half]
        vm_ref[...] = acc[:, half:]

    @pl.when(n == 2)
    def _():
        zm_ref[...] = _silu(acc).astype(BF16)

    @pl.when(n == 3)
    def _():
        qd_ref[...] = acc.astype(BF16)

    @pl.when(n == 4)
    def _():
        kd_ref[...] = acc[:, :half]
        vd_ref[...] = acc[:, half:]

    @pl.when(n == 5)
    def _():
        zd_ref[...] = _silu(acc).astype(BF16)

    @pl.when(n >= 6)
    def _():
        gt_ref[...] = _sigmoid(acc).astype(BF16)


def _proj(x2d, w_bf, tm):
    m_rows = x2d.shape[0]
    n_tiles = w_bf.shape[1] // PROJ_TN

    def out_map(lo, hi):
        return lambda m, n: (m, jnp.clip(n - lo, 0, hi - lo - 1))

    half = PROJ_TN // 2
    out_shape = (
        jax.ShapeDtypeStruct((m_rows, 1024), BF16),
        jax.ShapeDtypeStruct((m_rows, 512), F32),
        jax.ShapeDtypeStruct((m_rows, 512), F32),
        jax.ShapeDtypeStruct((m_rows, 1024), BF16),
        jax.ShapeDtypeStruct((m_rows, 1024), BF16),
        jax.ShapeDtypeStruct((m_rows, 512), F32),
        jax.ShapeDtypeStruct((m_rows, 512), F32),
        jax.ShapeDtypeStruct((m_rows, 1024), BF16),
        jax.ShapeDtypeStruct((m_rows, 4096), BF16),
    )
    out_specs = (
        pl.BlockSpec((tm, PROJ_TN), out_map(0, 1)),
        pl.BlockSpec((tm, half), out_map(1, 2)),
        pl.BlockSpec((tm, half), out_map(1, 2)),
        pl.BlockSpec((tm, PROJ_TN), out_map(2, 3)),
        pl.BlockSpec((tm, PROJ_TN), out_map(3, 4)),
        pl.BlockSpec((tm, half), out_map(4, 5)),
        pl.BlockSpec((tm, half), out_map(4, 5)),
        pl.BlockSpec((tm, PROJ_TN), out_map(5, 6)),
        pl.BlockSpec((tm, PROJ_TN), out_map(6, 10)),
    )
    return pl.pallas_call(
        _proj_kernel,
        grid=(m_rows // tm, n_tiles),
        in_specs=[pl.BlockSpec((tm, D_MODEL), lambda m, n: (m, 0)),
                  pl.BlockSpec((D_MODEL, PROJ_TN), lambda m, n: (0, n))],
        out_specs=out_specs,
        out_shape=out_shape,
        scratch_shapes=[pltpu.VMEM((tm, D_MODEL), BF16)],
        compiler_params=pltpu.CompilerParams(
            dimension_semantics=("parallel", "arbitrary"), vmem_limit_bytes=VMEM_LIMIT_V7X),
        name="proj",
    )(x2d, w_bf)


def _alibi_tiles(head0, slope_mul, rows, blk):
    r_i = lax.broadcasted_iota(jnp.int32, (rows, blk), 0)
    c_i = lax.broadcasted_iota(jnp.int32, (rows, blk), 1)
    g_row = r_i // blk
    slope = _pow2_neg(slope_mul * (head0 + g_row + 1))
    rel = (r_i - g_row * blk - c_i).astype(F32)
    return slope * rel, slope * float(blk), rel >= 0.0


def _moba_prompt_kernel(q_ref, k_ref, v_ref, z_ref, o_ref, kb_sc, vb_sc, km_sc, *, n_blk):
    hk = pl.program_id(1)
    j = pl.program_id(2)
    blk = MOBA_BLOCK
    rows = 2 * blk

    @pl.when(j == 0)
    def _():
        k = k_ref[0]
        kb_sc[...] = k.astype(BF16)
        vb_sc[...] = v_ref[0].astype(BF16)
        km_sc[...] = jnp.sum(k.reshape(n_blk, blk, HEAD_DIM), axis=1) * (1.0 / blk)

    bias0, slope_blk, causal = _alibi_tiles(2 * hk, 1, rows, blk)
    qj = q_ref[0]
    qs = jnp.concatenate([qj[:, :HEAD_DIM], qj[:, HEAD_DIM:]], axis=0)

    kmean = km_sc[...]
    kmean_hi = kmean.astype(BF16)
    kmean_lo = (kmean - kmean_hi.astype(F32)).astype(BF16)
    gate = (lax.dot_general(qs, kmean_hi, _NT, preferred_element_type=F32)
            + lax.dot_general(qs, kmean_lo, _NT, preferred_element_type=F32))
    lane8 = lax.broadcasted_iota(jnp.int32, (rows, n_blk), 1)
    gate = jnp.where(lane8 < j, gate, NEG_INF)
    rank = jnp.zeros((rows, n_blk), F32)
    for m in range(n_blk):
        col = gate[:, m:m + 1]
        ge = jnp.where(col >= gate, 1.0, 0.0)
        gt = jnp.where(col > gate, 1.0, 0.0)
        rank = rank + jnp.where(lane8 > m, ge, gt)

    off_j = pl.multiple_of(j * blk, blk)
    s = lax.dot_general(qs, kb_sc[pl.ds(off_j, blk), :], _NT, preferred_element_type=F32) * SCALE - bias0
    s = jnp.where(causal, s, NEG_INF)
    m0 = jnp.max(s, axis=1, keepdims=True)
    p = jnp.exp(s - m0)
    l0 = jnp.sum(p, axis=1, keepdims=True)
    acc0 = jnp.dot(p.astype(BF16), vb_sc[pl.ds(off_j, blk), :], preferred_element_type=F32)

    def body(n, carry):
        m_run, l_run, acc = carry
        off = pl.multiple_of(n * blk, blk)
        s = lax.dot_general(qs, kb_sc[pl.ds(off, blk), :], _NT, preferred_element_type=F32) * SCALE
        s = s - (bias0 + slope_blk * (j - n).astype(F32))
        rk = jnp.sum(jnp.where(lane8 == n, rank, 0.0), axis=1, keepdims=True)
        s = jnp.where(rk < float(MOBA_TOPK), s, NEG_INF)
        m_new = jnp.maximum(m_run, jnp.max(s, axis=1, keepdims=True))
        alpha = jnp.exp(m_run - m_new)
        p = jnp.exp(s - m_new)
        l_new = alpha * l_run + jnp.sum(p, axis=1, keepdims=True)
        acc = alpha * acc + jnp.dot(p.astype(BF16), vb_sc[pl.ds(off, blk), :], preferred_element_type=F32)
        return m_new, l_new, acc

    _, l_fin, acc = lax.fori_loop(0, j, body, (m0, l0, acc0))
    o = acc / l_fin
    o2 = jnp.concatenate([o[:blk], o[blk:]], axis=1)
    o_ref[0] = (o2 * z_ref[0].astype(F32)).astype(BF16)


def _moba_prompt(qm, km, vm, zm, batch, seq):
    n_blk = seq // MOBA_BLOCK
    blk = MOBA_BLOCK
    q3 = qm.reshape(batch, seq, MOBA_HEADS * HEAD_DIM)
    k3 = km.reshape(batch, seq, MOBA_KV_HEADS * HEAD_DIM)
    v3 = vm.reshape(batch, seq, MOBA_KV_HEADS * HEAD_DIM)
    z3 = zm.reshape(batch, seq, MOBA_HEADS * HEAD_DIM)
    out = pl.pallas_call(
        functools.partial(_moba_prompt_kernel, n_blk=n_blk),
        grid=(batch, MOBA_KV_HEADS, n_blk),
        in_specs=[pl.BlockSpec((1, blk, 2 * HEAD_DIM), lambda b, h, j: (b, j, h)),
                  pl.BlockSpec((1, seq, HEAD_DIM), lambda b, h, j: (b, 0, h)),
                  pl.BlockSpec((1, seq, HEAD_DIM), lambda b, h, j: (b, 0, h)),
                  pl.BlockSpec((1, blk, 2 * HEAD_DIM), lambda b, h, j: (b, j, h))],
        out_specs=pl.BlockSpec((1, blk, 2 * HEAD_DIM), lambda b, h, j: (b, j, h)),
        out_shape=jax.ShapeDtypeStruct((batch, seq, MOBA_HEADS * HEAD_DIM), BF16),
        scratch_shapes=[pltpu.VMEM((seq, HEAD_DIM), BF16), pltpu.VMEM((seq, HEAD_DIM), BF16),
                        pltpu.VMEM((n_blk, HEAD_DIM), F32)],
        compiler_params=pltpu.CompilerParams(
            dimension_semantics=("parallel", "parallel", "arbitrary"), vmem_limit_bytes=VMEM_LIMIT_V7X),
        name="moba_prompt",
    )(q3, k3, v3, z3)
    return out.reshape(batch * seq, MOBA_HEADS * HEAD_DIM)


def _lambda_full(lq1_ref, lk1_ref, lq2_ref, lk2_ref):
    a = jnp.sum(lq1_ref[...] * lk1_ref[...], axis=1, keepdims=True)
    b = jnp.sum(lq2_ref[...] * lk2_ref[...], axis=1, keepdims=True)
    return jnp.exp(a) - jnp.exp(b) + LAM0


def _diff_prompt_kernel(q_ref, k_ref, v_ref, z_ref, gn_ref, lq1_ref, lk1_ref, lq2_ref, lk2_ref, o_ref,
                        kb_sc, vb_sc):
    hk = pl.program_id(1)
    j = pl.program_id(2)
    blk = MOBA_BLOCK
    rows = 2 * blk

    @pl.when(j == 0)
    def _():
        kb_sc[...] = k_ref[0].astype(BF16)
        vb_sc[...] = v_ref[0].astype(BF16)

    lam = _lambda_full(lq1_ref, lk1_ref, lq2_ref, lk2_ref)
    bias0, slope_blk, causal = _alibi_tiles(2 * hk, 2, rows, blk)
    qj = q_ref[0]
    q_maps = (jnp.concatenate([qj[:, 0:128], qj[:, 256:384]], axis=0),
              jnp.concatenate([qj[:, 128:256], qj[:, 384:512]], axis=0))

    def scores(mi, off, bias):
        kn = kb_sc[pl.ds(off, blk), mi * HEAD_DIM:(mi + 1) * HEAD_DIM]
        return lax.dot_general(q_maps[mi], kn, _NT, preferred_element_type=F32) * SCALE - bias

    off_j = pl.multiple_of(j * blk, blk)
    vj = vb_sc[pl.ds(off_j, blk), :]
    init = []
    for mi in range(2):
        s = jnp.where(causal, scores(mi, off_j, bias0), NEG_INF)
        m0 = jnp.max(s, axis=1, keepdims=True)
        p = jnp.exp(s - m0)
        init += [m0, jnp.sum(p, axis=1, keepdims=True), jnp.dot(p.astype(BF16), vj, preferred_element_type=F32)]

    def body(n, carry):
        off = pl.multiple_of(n * blk, blk)
        bias = bias0 + slope_blk * (j - n).astype(F32)
        vn = vb_sc[pl.ds(off, blk), :]
        out = []
        for mi in range(2):
            m_run, l_run, acc = carry[3 * mi:3 * mi + 3]
            s = scores(mi, off, bias)
            m_new = jnp.maximum(m_run, jnp.max(s, axis=1, keepdims=True))
            alpha = jnp.exp(m_run - m_new)
            p = jnp.exp(s - m_new)
            out += [m_new, alpha * l_run + jnp.sum(p, axis=1, keepdims=True),
                    alpha * acc + jnp.dot(p.astype(BF16), vn, preferred_element_type=F32)]
        return tuple(out)

    _, l1, acc1, _, l2, acc2 = lax.fori_loop(0, j, body, tuple(init))
    o = acc1 / l1 - lam * (acc2 / l2)
    ms = jnp.mean(o * o, axis=1, keepdims=True)
    o = o * lax.rsqrt(ms + RMS_EPS)
    gn = gn_ref[0]
    o2 = jnp.concatenate([o[:blk] * gn[0:1, :], o[blk:] * gn[1:2, :]], axis=1) * (1.0 - LAM0)
    o_ref[0] = (o2 * z_ref[0].astype(F32)).astype(BF16)


def _diff_prompt(qd, kd, vd, zd, gn3, lams, batch, seq):
    n_blk = seq // MOBA_BLOCK
    blk = MOBA_BLOCK
    q3 = qd.reshape(batch, seq, 1024)
    k3 = kd.reshape(batch, seq, 512)
    v3 = vd.reshape(batch, seq, 512)
    z3 = zd.reshape(batch, seq, 1024)
    vec = pl.BlockSpec((1, HEAD_DIM), lambda b, h, j: (0, 0))
    out = pl.pallas_call(
        _diff_prompt_kernel,
        grid=(batch, DIFF_KV_HEADS, n_blk),
        in_specs=[pl.BlockSpec((1, blk, 512), lambda b, h, j: (b, j, h)),
                  pl.BlockSpec((1, seq, 256), lambda b, h, j: (b, 0, h)),
                  pl.BlockSpec((1, seq, 256), lambda b, h, j: (b, 0, h)),
                  pl.BlockSpec((1, blk, 512), lambda b, h, j: (b, j, h)),
                  pl.BlockSpec((1, 2, DIFF_V_DIM), lambda b, h, j: (h, 0, 0)),
                  vec, vec, vec, vec],
        out_specs=pl.BlockSpec((1, blk, 512), lambda b, h, j: (b, j, h)),
        out_shape=jax.ShapeDtypeStruct((batch, seq, 1024), BF16),
        scratch_shapes=[pltpu.VMEM((seq, 256), BF16), pltpu.VMEM((seq, 256), BF16)],
        compiler_params=pltpu.CompilerParams(
            dimension_semantics=("parallel", "parallel", "arbitrary"), vmem_limit_bytes=VMEM_LIMIT_V7X),
        name="diff_prompt",
    )(q3, k3, v3, z3, gn3, *lams)
    return out.reshape(batch * seq, 1024)


def _pair_lo(x, even):
    return jnp.where(even, x, pltpu.roll(x, 1, axis=0))


def _pair_hi(x, even):
    return jnp.where(even, pltpu.roll(x, x.shape[0] - 1, axis=0), x)


def _dec1_kernel(pt_ref, qd_ref, qm_ref, kdn_ref, vdn_ref, zd_ref, gn_ref, lq1_ref, lk1_ref, lq2_ref, lk2_ref,
                 ck_ref, cv_ref, cmk_ref, ad_ref, sel_ref, m_sc, l_sc, acc1_sc, acc2_sc, ks_sc,
                 *, n_pages, past_len):
    i = pl.program_id(1)
    rows_pg = PAGE_SIZE * 4

    @pl.when(i == 0)
    def _():
        m_sc[...] = jnp.full(m_sc.shape, NEG_INF, F32)
        l_sc[...] = jnp.zeros(l_sc.shape, F32)
        acc1_sc[...] = jnp.zeros(acc1_sc.shape, F32)
        acc2_sc[...] = jnp.zeros(acc2_sc.shape, F32)

    q8 = qd_ref[0]
    row = lax.broadcasted_iota(jnp.int32, (8, rows_pg), 0)
    col = lax.broadcasted_iota(jnp.int32, (8, rows_pg), 1)
    row1 = lax.broadcasted_iota(jnp.int32, (8, HEAD_DIM), 0)
    even = (row & 1) == 0
    even1 = (row1 & 1) == 0

    s = lax.dot_general(q8, ck_ref[0].astype(BF16), _NT, preferred_element_type=F32) * SCALE
    slope = _pow2_neg(2 * ((row >> 1) + 1))
    dist = (past_len - (i * PAGE_SIZE + (col >> 2))).astype(F32)
    s = s - slope * dist
    valid = (col & 3) == ((row & 1) * 2 + (row >> 2))
    s = jnp.where(valid, s, NEG_INF)
    m_old = m_sc[...]
    m_new = jnp.maximum(m_old, jnp.max(s, axis=1, keepdims=True))
    alpha = jnp.exp(m_old - m_new)
    p = jnp.where(valid, jnp.exp(s - m_new[:, :1]), 0.0)
    l_sc[...] = alpha * l_sc[...] + jnp.sum(p, axis=1, keepdims=True)
    m_sc[...] = m_new
    p_r = pltpu.roll(p, 2, axis=1)
    p_l = pltpu.roll(p, rows_pg - 2, axis=1)
    w1 = jnp.where(even, p, pltpu.roll(p_r, 1, axis=0)).astype(BF16)
    w2 = jnp.where(even, pltpu.roll(p_l, 7, axis=0), p).astype(BF16)
    vb = cv_ref[0].astype(BF16)
    acc1_sc[...] = _pair_lo(alpha, even1) * acc1_sc[...] + jnp.dot(w1, vb, preferred_element_type=F32)
    acc2_sc[...] = _pair_hi(alpha, even1) * acc2_sc[...] + jnp.dot(w2, vb, preferred_element_type=F32)

    ksum = jnp.sum(cmk_ref[0].reshape(rows_pg // 8, 8, HEAD_DIM), axis=0)
    blk_i = i // 2

    @pl.when(i % 2 == 0)
    def _():
        ks_sc[pl.ds(blk_i, 1)] = ksum[None]

    @pl.when(i % 2 == 1)
    def _():
        ks_sc[pl.ds(blk_i, 1)] = ks_sc[pl.ds(blk_i, 1)] + ksum[None]

    @pl.when(i == n_pages - 1)
    def _():
        lam = _lambda_full(lq1_ref, lk1_ref, lq2_ref, lk2_ref)
        qf = q8.astype(F32)
        s_new = jnp.sum(qf * kdn_ref[0], axis=1, keepdims=True) * SCALE
        m_o = m_sc[...]
        m_f = jnp.maximum(m_o, s_new)
        a_f = jnp.exp(m_o - m_f)
        p_new = jnp.exp(s_new - m_f)
        l_f = a_f * l_sc[...] + p_new
        vdn = vdn_ref[0]
        acc1 = _pair_lo(a_f, even1) * acc1_sc[...] + _pair_lo(p_new, even1) * vdn
        acc2 = _pair_hi(a_f, even1) * acc2_sc[...] + _pair_hi(p_new, even1) * vdn
        o = acc1 / _pair_lo(l_f, even1) - lam * (acc2 / _pair_hi(l_f, even1))
        ss = jnp.sum(o * o, axis=1, keepdims=True)
        ss = jnp.broadcast_to(ss, (8, HEAD_DIM))
        ss = ss + jnp.where(even1, pltpu.roll(ss, 7, axis=0), pltpu.roll(ss, 1, axis=0))
        o = o * lax.rsqrt(ss * (1.0 / DIFF_V_DIM) + RMS_EPS) * gn_ref[...] * (1.0 - LAM0)
        ad_ref[0] = (o * zd_ref[0].astype(F32)).astype(BF16)

        n_cols = ks_sc.shape[0] * 8
        ks = ks_sc[...].reshape(n_cols, HEAD_DIM)
        ks_hi = ks.astype(BF16)
        ks_lo = (ks - ks_hi.astype(F32)).astype(BF16)
        qm8 = qm_ref[0]
        g_all = (lax.dot_general(qm8, ks_hi, _NT, preferred_element_type=F32)
                 + lax.dot_general(qm8, ks_lo, _NT, preferred_element_type=F32)) * (1.0 / MOBA_BLOCK)
        g2 = g_all + pltpu.roll(g_all, n_cols - 4, axis=1)
        rw = lax.broadcasted_iota(jnp.int32, (8, n_cols), 0)
        cl = lax.broadcasted_iota(jnp.int32, (8, n_cols), 1)
        clf = cl.astype(F32)
        gm = jnp.where((cl & 7) == (rw >> 1), g2, NEG_INF)
        lane = lax.broadcasted_iota(jnp.int32, (8, HEAD_DIM), 1)
        sel = jnp.zeros((8, HEAD_DIM), jnp.int32)
        for t in range(MOBA_TOPK):
            mx = jnp.max(gm, axis=1, keepdims=True)
            idx = jnp.min(jnp.where(gm == mx, clf, float(n_cols)), axis=1, keepdims=True)
            gm = jnp.where(clf == idx, NEG_INF, gm)
            sel = jnp.where(lane == t, idx.astype(jnp.int32) >> 3, sel)
        sel_ref[0] = sel


def _dec1(page_table, qd8, qm8, kdn8, vdn8, zd8, gn8, lams, ck3, cv3, cmk3, past_len):
    dec_batch, n_pages = page_table.shape
    n_blocks = n_pages * PAGE_SIZE // MOBA_BLOCK
    rows_pg = PAGE_SIZE * 4
    row8 = lambda: pl.BlockSpec((1, 8, HEAD_DIM), lambda b, i, pt: (b, 0, 0))
    vec = pl.BlockSpec((1, HEAD_DIM), lambda b, i, pt: (0, 0))
    page = lambda: pl.BlockSpec((1, rows_pg, HEAD_DIM), lambda b, i, pt: (pt[b, i], 0, 0))
    grid_spec = pltpu.PrefetchScalarGridSpec(
        num_scalar_prefetch=1,
        grid=(dec_batch, n_pages),
        in_specs=[row8(), row8(), row8(), row8(), row8(),
                  pl.BlockSpec((8, HEAD_DIM), lambda b, i, pt: (0, 0)),
                  vec, vec, vec, vec,
                  page(), page(), page()],
        out_specs=(pl.BlockSpec((1, 8, HEAD_DIM), lambda b, i, pt: (b, 0, 0)),
                   pl.BlockSpec((1, 8, HEAD_DIM), lambda b, i, pt: (b, 0, 0))),
        scratch_shapes=[pltpu.VMEM((8, HEAD_DIM), F32), pltpu.VMEM((8, HEAD_DIM), F32),
                        pltpu.VMEM((8, HEAD_DIM), F32), pltpu.VMEM((8, HEAD_DIM), F32),
                        pltpu.VMEM((n_blocks, 8, HEAD_DIM), F32)],
    )
    return pl.pallas_call(
        functools.partial(_dec1_kernel, n_pages=n_pages, past_len=past_len),
        grid_spec=grid_spec,
        out_shape=(jax.ShapeDtypeStruct((dec_batch, 8, HEAD_DIM), BF16),
                   jax.ShapeDtypeStruct((dec_batch, 8, HEAD_DIM), jnp.int32)),
        compiler_params=pltpu.CompilerParams(
            dimension_semantics=("parallel", "arbitrary"), vmem_limit_bytes=VMEM_LIMIT_V7X),
        name="dec1",
    )(page_table, qd8, qm8, kdn8, vdn8, zd8, gn8, *lams, ck3, cv3, cmk3)


def _dec2_kernel(pt_ref, sel_ref, qm_ref, kmn_ref, vmn_ref, zm_ref, ck_ref, cv_ref, am_ref, m_sc, l_sc, acc_sc,
                 *, past_len):
    b = pl.program_id(0)
    h = pl.program_id(1)
    t = pl.program_id(2)
    hf = pl.program_id(3)
    rows_pg = PAGE_SIZE * 4

    @pl.when((h == 0) & (t == 0) & (hf == 0))
    def _():
        m_sc[...] = jnp.full(m_sc.shape, NEG_INF, F32)
        l_sc[...] = jnp.zeros(l_sc.shape, F32)
        acc_sc[...] = jnp.zeros(acc_sc.shape, F32)

    q8 = qm_ref[0]
    row = lax.broadcasted_iota(jnp.int32, (8, rows_pg), 0)
    col = lax.broadcasted_iota(jnp.int32, (8, rows_pg), 1)
    blk = sel_ref[b, h * MOBA_TOPK + t]
    s = lax.dot_general(q8, ck_ref[0].astype(BF16), _NT, preferred_element_type=F32) * SCALE
    slope = _pow2_neg(row + 1)
    dist = (past_len - (blk * MOBA_BLOCK + hf * PAGE_SIZE + (col >> 2))).astype(F32)
    s = s - slope * dist
    valid = (row == h) & ((col & 3) == (h >> 1))
    s = jnp.where(valid, s, NEG_INF)
    m_old = m_sc[...]
    m_new = jnp.maximum(m_old, jnp.max(s, axis=1, keepdims=True))
    alpha = jnp.exp(m_old - m_new)
    p = jnp.where(valid, jnp.exp(s - m_new[:, :1]), 0.0)
    l_sc[...] = alpha * l_sc[...] + jnp.sum(p, axis=1, keepdims=True)
    m_sc[...] = m_new
    acc_sc[...] = alpha * acc_sc[...] + jnp.dot(p.astype(BF16), cv_ref[0].astype(BF16),
                                                preferred_element_type=F32)

    @pl.when((h == MOBA_HEADS - 1) & (t == MOBA_TOPK - 1) & (hf == 1))
    def _():
        qf = q8.astype(F32)
        s_new = jnp.sum(qf * kmn_ref[0], axis=1, keepdims=True) * SCALE
        m_o = m_sc[...]
        m_f = jnp.maximum(m_o, s_new)
        a_f = jnp.exp(m_o - m_f)
        p_new = jnp.exp(s_new - m_f)
        l_f = a_f * l_sc[...] + p_new
        o = (a_f * acc_sc[...] + p_new * vmn_ref[0]) / l_f
        am_ref[0] = (o * zm_ref[0].astype(F32)).astype(BF16)


def _dec2(page_table, sel, qm8, kmn8, vmn8, zm8, cmk3, cmv3, past_len):
    dec_batch = page_table.shape[0]
    rows_pg = PAGE_SIZE * 4
    row8 = lambda: pl.BlockSpec((1, 8, HEAD_DIM), lambda b, h, t, f, pt, sl: (b, 0, 0))
    page = lambda: pl.BlockSpec(
        (1, rows_pg, HEAD_DIM), lambda b, h, t, f, pt, sl: (pt[b, sl[b, h * MOBA_TOPK + t] * 2 + f], 0, 0))
    grid_spec = pltpu.PrefetchScalarGridSpec(
        num_scalar_prefetch=2,
        grid=(dec_batch, MOBA_HEADS, MOBA_TOPK, MOBA_BLOCK // PAGE_SIZE),
        in_specs=[row8(), row8(), row8(), row8(), page(), page()],
        out_specs=pl.BlockSpec((1, 8, HEAD_DIM), lambda b, h, t, f, pt, sl: (b, 0, 0)),
        scratch_shapes=[pltpu.VMEM((8, HEAD_DIM), F32), pltpu.VMEM((8, HEAD_DIM), F32),
                        pltpu.VMEM((8, HEAD_DIM), F32)],
    )
    return pl.pallas_call(
        functools.partial(_dec2_kernel, past_len=past_len),
        grid_spec=grid_spec,
        out_shape=jax.ShapeDtypeStruct((dec_batch, 8, HEAD_DIM), BF16),
        compiler_params=pltpu.CompilerParams(
            dimension_semantics=("parallel", "arbitrary", "arbitrary", "arbitrary"),
            vmem_limit_bytes=VMEM_LIMIT_V7X),
        name="dec2",
    )(page_table, sel, qm8, kmn8, vmn8, zm8, cmk3, cmv3)


def _post_kernel(am_ref, ad_ref, g_ref, x_ref, wmo_ref, wdo_ref, wo_ref, lng_ref, lnb_ref, y_ref):
    ym = jnp.dot(am_ref[...], wmo_ref[...], preferred_element_type=F32)
    yd = jnp.dot(ad_ref[...], wdo_ref[...], preferred_element_type=F32)
    g = g_ref[...].astype(F32)
    merged = g[:, :D_MODEL] * ym + g[:, D_MODEL:] * yd
    f = jnp.dot(merged.astype(BF16), wo_ref[...], preferred_element_type=F32)
    u = ALPHA * x_ref[...] + f
    mu = jnp.mean(u, axis=1, keepdims=True)
    d = u - mu
    var = jnp.mean(d * d, axis=1, keepdims=True)
    y_ref[...] = d * lax.rsqrt(var + LN_EPS) * lng_ref[...] + lnb_ref[...]


def _post(am, ad, gt, x2d, wmo, wdo, wo, lng, lnb, tm):
    m_rows = x2d.shape[0]
    rowblk = lambda w: pl.BlockSpec((tm, w), lambda m: (m, 0))
    full = lambda a: pl.BlockSpec(a.shape, lambda m: (0, 0))
    return pl.pallas_call(
        _post_kernel,
        grid=(m_rows // tm,),
        in_specs=[rowblk(1024), rowblk(1024), rowblk(N_BRANCH * D_MODEL), rowblk(D_MODEL),
                  full(wmo), full(wdo), full(wo), full(lng), full(lnb)],
        out_specs=rowblk(D_MODEL),
        out_shape=jax.ShapeDtypeStruct((m_rows, D_MODEL), F32),
        compiler_params=pltpu.CompilerParams(
            dimension_semantics=("parallel",), vmem_limit_bytes=VMEM_LIMIT_V7X),
        name="post",
    )(am, ad, gt, x2d, wmo, wdo, wo, lng, lnb)


def kernel(x_prompt, x_sample, cache_moba_k, cache_moba_v, cache_diff_k, cache_diff_v, page_table, w_in, w_moba_o, w_diff_o, w_out, lambda_q1, lambda_k1, lambda_q2, lambda_k2, diff_norm_g, ln_g, ln_b):
    assert w_in.shape[0] == DEPTH == 1
    batch, seq, _ = x_prompt.shape
    dec_batch, dec_seq, _ = x_sample.shape
    assert dec_seq == 1 and seq % MOBA_BLOCK == 0
    n_pages = page_table.shape[1]
    past_len = n_pages * PAGE_SIZE
    assert past_len % MOBA_BLOCK == 0 and past_len // MOBA_BLOCK >= MOBA_TOPK
    n_phys = cache_moba_k.shape[1]

    w_bf = w_in[0].astype(BF16)
    wmo = w_moba_o[0].astype(BF16)
    wdo = w_diff_o[0].astype(BF16)
    wo = w_out[0].astype(BF16)
    lams = (lambda_q1, lambda_k1, lambda_q2, lambda_k2)
    lng = ln_g.reshape(1, D_MODEL)
    lnb = ln_b.reshape(1, D_MODEL)

    xp = x_prompt.reshape(batch * seq, D_MODEL)
    qm, km, vm, zm, qd, kd, vd, zd, gt = _proj(xp, w_bf, tm=512)
    am = _moba_prompt(qm, km, vm, zm, batch, seq)
    ad = _diff_prompt(qd, kd, vd, zd, diff_norm_g.reshape(DIFF_KV_HEADS, 2, DIFF_V_DIM), lams, batch, seq)
    y_p = _post(am, ad, gt, xp, wmo, wdo, wo, lng, lnb, tm=256)

    xs = x_sample.reshape(dec_batch, D_MODEL)
    qm_s, km_s, vm_s, zm_s, qd_s, kd_s, vd_s, zd_s, gt_s = _proj(xs, w_bf, tm=dec_batch)
    rows_pg = PAGE_SIZE * 4
    cmk3 = cache_moba_k.reshape(n_phys, rows_pg, HEAD_DIM)
    cmv3 = cache_moba_v.reshape(n_phys, rows_pg, HEAD_DIM)

    def diff_pages(c):
        c = c.reshape(n_phys, PAGE_SIZE, DIFF_KV_HEADS, 2, HEAD_DIM)
        return jnp.transpose(c, (0, 1, 3, 2, 4)).reshape(n_phys, rows_pg, HEAD_DIM)

    cdk3 = diff_pages(cache_diff_k)
    cdv3 = diff_pages(cache_diff_v)

    def pair_rows(a):
        a = a.reshape(dec_batch, DIFF_KV_HEADS, 1, 2, HEAD_DIM)
        return jnp.broadcast_to(a, (dec_batch, DIFF_KV_HEADS, 2, 2, HEAD_DIM)).reshape(dec_batch, 8, HEAD_DIM)

    qd8 = qd_s.reshape(dec_batch, 8, HEAD_DIM)
    qm8 = qm_s.reshape(dec_batch, 8, HEAD_DIM)
    zd8 = zd_s.reshape(dec_batch, 8, HEAD_DIM)
    zm8 = zm_s.reshape(dec_batch, 8, HEAD_DIM)
    gn8 = diff_norm_g.reshape(8, HEAD_DIM)
    ad_s, sel = _dec1(page_table, qd8, qm8, pair_rows(kd_s), pair_rows(vd_s), zd8, gn8, lams,
                      cdk3, cdv3, cmk3, past_len)
    sel2 = sel[:, :, :MOBA_TOPK].reshape(dec_batch, MOBA_HEADS * MOBA_TOPK)
    kmn8 = jnp.repeat(km_s.reshape(dec_batch, MOBA_KV_HEADS, HEAD_DIM), 2, axis=1)
    vmn8 = jnp.repeat(vm_s.reshape(dec_batch, MOBA_KV_HEADS, HEAD_DIM), 2, axis=1)
    am_s = _dec2(page_table, sel2, qm8, kmn8, vmn8, zm8, cmk3, cmv3, past_len)
    y_s = _post(am_s.reshape(dec_batch, 1024), ad_s.reshape(dec_batch, 1024), gt_s, xs,
                wmo, wdo, wo, lng, lnb, tm=dec_batch)

    return (y_p.reshape(batch, seq, D_MODEL), y_s.reshape(dec_batch, 1, D_MODEL),
            km.reshape(DEPTH, batch, seq, MOBA_KV_HEADS, HEAD_DIM),
            vm.reshape(DEPTH, batch, seq, MOBA_KV_HEADS, HEAD_DIM),
            kd.reshape(DEPTH, batch, seq, DIFF_KV_HEADS, 2 * HEAD_DIM),
            vd.reshape(DEPTH, batch, seq, DIFF_KV_HEADS, DIFF_V_DIM),
            km_s.reshape(DEPTH, dec_batch, 1, MOBA_KV_HEADS, HEAD_DIM),
            vm_s.reshape(DEPTH, dec_batch, 1, MOBA_KV_HEADS, HEAD_DIM),
            kd_s.reshape(DEPTH, dec_batch, 1, DIFF_KV_HEADS, 2 * HEAD_DIM),
            vd_s.reshape(DEPTH, dec_batch, 1, DIFF_KV_HEADS, DIFF_V_DIM))
```

```python
import functools
import math

import jax
import jax.numpy as jnp
from jax import lax
from jax.experimental import pallas as pl
from jax.experimental.pallas import tpu as pltpu

F32 = jnp.float32
BF16 = jnp.bfloat16

D_MODEL = 2048
HEAD_DIM = 128
PAGE_SIZE = 128
MOBA_HEADS = 8
MOBA_KV_HEADS = 4
MOBA_BLOCK = 256
MOBA_TOPK = 3
DIFF_HEADS = 4
DIFF_KV_HEADS = 2
DIFF_V_DIM = 2 * HEAD_DIM
N_BRANCH = 2
DEPTH = 1
ALPHA = (2 * DEPTH) ** 0.25
LN_EPS = 1e-5
RMS_EPS = 1e-5
NEG_INF = -1e30
SCALE = HEAD_DIM ** -0.5
LAM0 = 0.8 - 0.6 * math.exp(-0.3 * 0)

VMEM_LIMIT_V7X = 56 * 1024 * 1024

PROJ_TN = 1024

_NT = (((1,), (1,)), ((), ()))


def _pow2_neg(k):
    return lax.bitcast_convert_type((127 - k) << 23, F32)


def _silu(v):
    return v * (1.0 / (1.0 + jnp.exp(-v)))


def _sigmoid(v):
    return 1.0 / (1.0 + jnp.exp(-v))


def _proj_kernel(x_ref, w_ref, qm_ref, km_ref, vm_ref, zm_ref, qd_ref, kd_ref, vd_ref, zd_ref, gt_ref, xb_ref):
    n = pl.program_id(1)

    @pl.when(n == 0)
    def _():
        xb_ref[...] = x_ref[...].astype(BF16)

    acc = jnp.dot(xb_ref[...], w_ref[...], preferred_element_type=F32)
    half = PROJ_TN // 2

    @pl.when(n == 0)
    def _():
        qm_ref[...] = acc.astype(BF16)

    @pl.when(n == 1)
    def _():
        km_ref[...] = acc[:, :half]
        vm_ref[...] = acc[:, half:]

    @pl.when(n == 2)
    def _():
        zm_ref[...] = _silu(acc).astype(BF16)

    @pl.when(n == 3)
    def _():
        qd_ref[...] = acc.astype(BF16)

    @pl.when(n == 4)
    def _():
        kd_ref[...] = acc[:, :half]
        vd_ref[...] = acc[:, half:]

    @pl.when(n == 5)
    def _():
        zd_ref[...] = _silu(acc).astype(BF16)

    @pl.when(n >= 6)
    def _():
        gt_ref[...] = _sigmoid(acc).astype(BF16)


def _proj(x2d, w_bf, tm):
    m_rows = x2d.shape[0]
    n_tiles = w_bf.shape[1] // PROJ_TN

    def out_map(lo, hi):
        return lambda m, n: (m, jnp.clip(n - lo, 0, hi - lo - 1))

    half = PROJ_TN // 2
    out_shape = (
        jax.ShapeDtypeStruct((m_rows, 1024), BF16),
        jax.ShapeDtypeStruct((m_rows, 512), F32),
        jax.ShapeDtypeStruct((m_rows, 512), F32),
        jax.ShapeDtypeStruct((m_rows, 1024), BF16),
        jax.ShapeDtypeStruct((m_rows, 1024), BF16),
        jax.ShapeDtypeStruct((m_rows, 512), F32),
        jax.ShapeDtypeStruct((m_rows, 512), F32),
        jax.ShapeDtypeStruct((m_rows, 1024), BF16),
        jax.ShapeDtypeStruct((m_rows, 4096), BF16),
    )
    out_specs = (
        pl.BlockSpec((tm, PROJ_TN), out_map(0, 1)),
        pl.BlockSpec((tm, half), out_map(1, 2)),
        pl.BlockSpec((tm, half), out_map(1, 2)),
        pl.BlockSpec((tm, PROJ_TN), out_map(2, 3)),
        pl.BlockSpec((tm, PROJ_TN), out_map(3, 4)),
        pl.BlockSpec((tm, half), out_map(4, 5)),
        pl.BlockSpec((tm, half), out_map(4, 5)),
        pl.BlockSpec((tm, PROJ_TN), out_map(5, 6)),
        pl.BlockSpec((tm, PROJ_TN), out_map(6, 10)),
    )
    return pl.pallas_call(
        _proj_kernel,
        grid=(m_rows // tm, n_tiles),
        in_specs=[pl.BlockSpec((tm, D_MODEL), lambda m, n: (m, 0)),
                  pl.BlockSpec((D_MODEL, PROJ_TN), lambda m, n: (0, n))],
        out_specs=out_specs,
        out_shape=out_shape,
        scratch_shapes=[pltpu.VMEM((tm, D_MODEL), BF16)],
        compiler_params=pltpu.CompilerParams(
            dimension_semantics=("parallel", "arbitrary"), vmem_limit_bytes=VMEM_LIMIT_V7X),
        name="proj",
    )(x2d, w_bf)


def _alibi_tiles(head0, slope_mul, rows, blk):
    r_i = lax.broadcasted_iota(jnp.int32, (rows, blk), 0)
    c_i = lax.broadcasted_iota(jnp.int32, (rows, blk), 1)
    g_row = r_i // blk
    slope = _pow2_neg(slope_mul * (head0 + g_row + 1))
    rel = (r_i - g_row * blk - c_i).astype(F32)
    return slope * rel, slope * float(blk), rel >= 0.0


def _moba_prompt_kernel(q_ref, k_ref, v_ref, z_ref, o_ref, kb_sc, vb_sc, km_sc, *, n_blk):
    hk = pl.program_id(1)
    j = pl.program_id(2)
    blk = MOBA_BLOCK
    rows = 2 * blk

    @pl.when(j == 0)
    def _():
        k = k_ref[0]
        kb_sc[...] = k.astype(BF16)
        vb_sc[...] = v_ref[0].astype(BF16)
        km_sc[...] = jnp.sum(k.reshape(n_blk, blk, HEAD_DIM), axis=1) * (1.0 / blk)

    bias0, slope_blk, causal = _alibi_tiles(2 * hk, 1, rows, blk)
    qj = q_ref[0]
    qs = jnp.concatenate([qj[:, :HEAD_DIM], qj[:, HEAD_DIM:]], axis=0)

    kmean = km_sc[...]
    kmean_hi = kmean.astype(BF16)
    kmean_lo = (kmean - kmean_hi.astype(F32)).astype(BF16)
    gate = (lax.dot_general(qs, kmean_hi, _NT, preferred_element_type=F32)
            + lax.dot_general(qs, kmean_lo, _NT, preferred_element_type=F32))
    lane8 = lax.broadcasted_iota(jnp.int32, (rows, n_blk), 1)
    gate = jnp.where(lane8 < j, gate, NEG_INF)
    rank = jnp.zeros((rows, n_blk), F32)
    for m in range(n_blk):
        col = gate[:, m:m + 1]
        ge = jnp.where(col >= gate, 1.0, 0.0)
        gt = jnp.where(col > gate, 1.0, 0.0)
        rank = rank + jnp.where(lane8 > m, ge, gt)

    off_j = pl.multiple_of(j * blk, blk)
    s = lax.dot_general(qs, kb_sc[pl.ds(off_j, blk), :], _NT, preferred_element_type=F32) * SCALE - bias0
    s = jnp.where(causal, s, NEG_INF)
    m0 = jnp.max(s, axis=1, keepdims=True)
    p = jnp.exp(s - m0)
    l0 = jnp.sum(p, axis=1, keepdims=True)
    acc0 = jnp.dot(p.astype(BF16), vb_sc[pl.ds(off_j, blk), :], preferred_element_type=F32)

    def body(n, carry):
        m_run, l_run, acc = carry
        off = pl.multiple_of(n * blk, blk)
        s = lax.dot_general(qs, kb_sc[pl.ds(off, blk), :], _NT, preferred_element_type=F32) * SCALE
        s = s - (bias0 + slope_blk * (j - n).astype(F32))
        rk = jnp.sum(jnp.where(lane8 == n, rank, 0.0), axis=1, keepdims=True)
        s = jnp.where(rk < float(MOBA_TOPK), s, NEG_INF)
        m_new = jnp.maximum(m_run, jnp.max(s, axis=1, keepdims=True))
        alpha = jnp.exp(m_run - m_new)
        p = jnp.exp(s - m_new)
        l_new = alpha * l_run + jnp.sum(p, axis=1, keepdims=True)
        acc = alpha * acc + jnp.dot(p.astype(BF16), vb_sc[pl.ds(off, blk), :], preferred_element_type=F32)
        return m_new, l_new, acc

    _, l_fin, acc = lax.fori_loop(0, j, body, (m0, l0, acc0))
    o = acc / l_fin
    o2 = jnp.concatenate([o[:blk], o[blk:]], axis=1)
    o_ref[0] = (o2 * z_ref[0].astype(F32)).astype(BF16)


def _moba_prompt(qm, km, vm, zm, batch, seq):
    n_blk = seq // MOBA_BLOCK
    blk = MOBA_BLOCK
    q3 = qm.reshape(batch, seq, MOBA_HEADS * HEAD_DIM)
    k3 = km.reshape(batch, seq, MOBA_KV_HEADS * HEAD_DIM)
    v3 = vm.reshape(batch, seq, MOBA_KV_HEADS * HEAD_DIM)
    z3 = zm.reshape(batch, seq, MOBA_HEADS * HEAD_DIM)
    out = pl.pallas_call(
        functools.partial(_moba_prompt_kernel, n_blk=n_blk),
        grid=(batch, MOBA_KV_HEADS, n_blk),
        in_specs=[pl.BlockSpec((1, blk, 2 * HEAD_DIM), lambda b, h, j: (b, j, h)),
                  pl.BlockSpec((1, seq, HEAD_DIM), lambda b, h, j: (b, 0, h)),
                  pl.BlockSpec((1, seq, HEAD_DIM), lambda b, h, j: (b, 0, h)),
                  pl.BlockSpec((1, blk, 2 * HEAD_DIM), lambda b, h, j: (b, j, h))],
        out_specs=pl.BlockSpec((1, blk, 2 * HEAD_DIM), lambda b, h, j: (b, j, h)),
        out_shape=jax.ShapeDtypeStruct((batch, seq, MOBA_HEADS * HEAD_DIM), BF16),
        scratch_shapes=[pltpu.VMEM((seq, HEAD_DIM), BF16), pltpu.VMEM((seq, HEAD_DIM), BF16),
                        pltpu.VMEM((n_blk, HEAD_DIM), F32)],
        compiler_params=pltpu.CompilerParams(
            dimension_semantics=("parallel", "parallel", "arbitrary"), vmem_limit_bytes=VMEM_LIMIT_V7X),
        name="moba_prompt",
    )(q3, k3, v3, z3)
    return out.reshape(batch * seq, MOBA_HEADS * HEAD_DIM)


def _lambda_full(lq1_ref, lk1_ref, lq2_ref, lk2_ref):
    a = jnp.sum(lq1_ref[...] * lk1_ref[...], axis=1, keepdims=True)
    b = jnp.sum(lq2_ref[...] * lk2_ref[...], axis=1, keepdims=True)
    return jnp.exp(a) - jnp.exp(b) + LAM0


def _diff_prompt_kernel(q_ref, k_ref, v_ref, z_ref, gn_ref, lq1_ref, lk1_ref, lq2_ref, lk2_ref, o_ref,
                        kb_sc, vb_sc):
    hk = pl.program_id(1)
    j = pl.program_id(2)
    blk = MOBA_BLOCK
    rows = 2 * blk

    @pl.when(j == 0)
    def _():
        kb_sc[...] = k_ref[0].astype(BF16)
        vb_sc[...] = v_ref[0].astype(BF16)

    lam = _lambda_full(lq1_ref, lk1_ref, lq2_ref, lk2_ref)
    bias0, slope_blk, causal = _alibi_tiles(2 * hk, 2, rows, blk)
    qj = q_ref[0]
    q_maps = (jnp.concatenate([qj[:, 0:128], qj[:, 256:384]], axis=0),
              jnp.concatenate([qj[:, 128:256], qj[:, 384:512]], axis=0))

    def scores(mi, off, bias):
        kn = kb_sc[pl.ds(off, blk), mi * HEAD_DIM:(mi + 1) * HEAD_DIM]
        return lax.dot_general(q_maps[mi], kn, _NT, preferred_element_type=F32) * SCALE - bias

    off_j = pl.multiple_of(j * blk, blk)
    vj = vb_sc[pl.ds(off_j, blk), :]
    init = []
    for mi in range(2):
        s = jnp.where(causal, scores(mi, off_j, bias0), NEG_INF)
        m0 = jnp.max(s, axis=1, keepdims=True)
        p = jnp.exp(s - m0)
        init += [m0, jnp.sum(p, axis=1, keepdims=True), jnp.dot(p.astype(BF16), vj, preferred_element_type=F32)]

    def body(n, carry):
        off = pl.multiple_of(n * blk, blk)
        bias = bias0 + slope_blk * (j - n).astype(F32)
        vn = vb_sc[pl.ds(off, blk), :]
        out = []
        for mi in range(2):
            m_run, l_run, acc = carry[3 * mi:3 * mi + 3]
            s = scores(mi, off, bias)
            m_new = jnp.maximum(m_run, jnp.max(s, axis=1, keepdims=True))
            alpha = jnp.exp(m_run - m_new)
            p = jnp.exp(s - m_new)
            out += [m_new, alpha * l_run + jnp.sum(p, axis=1, keepdims=True),
                    alpha * acc + jnp.dot(p.astype(BF16), vn, preferred_element_type=F32)]
        return tuple(out)

    _, l1, acc1, _, l2, acc2 = lax.fori_loop(0, j, body, tuple(init))
    o = acc1 / l1 - lam * (acc2 / l2)
    ms = jnp.mean(o * o, axis=1, keepdims=True)
    o = o * lax.rsqrt(ms + RMS_EPS)
    gn = gn_ref[0]
    o2 = jnp.concatenate([o[:blk] * gn[0:1, :], o[blk:] * gn[1:2, :]], axis=1) * (1.0 - LAM0)
    o_ref[0] = (o2 * z_ref[0].astype(F32)).astype(BF16)


def _diff_prompt(qd, kd, vd, zd, gn3, lams, batch, seq):
    n_blk = seq // MOBA_BLOCK
    blk = MOBA_BLOCK
    q3 = qd.reshape(batch, seq, 1024)
    k3 = kd.reshape(batch, seq, 512)
    v3 = vd.reshape(batch, seq, 512)
    z3 = zd.reshape(batch, seq, 1024)
    vec = pl.BlockSpec((1, HEAD_DIM), lambda b, h, j: (0, 0))
    out = pl.pallas_call(
        _diff_prompt_kernel,
        grid=(batch, DIFF_KV_HEADS, n_blk),
        in_specs=[pl.BlockSpec((1, blk, 512), lambda b, h, j: (b, j, h)),
                  pl.BlockSpec((1, seq, 256), lambda b, h, j: (b, 0, h)),
                  pl.BlockSpec((1, seq, 256), lambda b, h, j: (b, 0, h)),
                  pl.BlockSpec((1, blk, 512), lambda b, h, j: (b, j, h)),
                  pl.BlockSpec((1, 2, DIFF_V_DIM), lambda b, h, j: (h, 0, 0)),
                  vec, vec, vec, vec],
        out_specs=pl.BlockSpec((1, blk, 512), lambda b, h, j: (b, j, h)),
        out_shape=jax.ShapeDtypeStruct((batch, seq, 1024), BF16),
        scratch_shapes=[pltpu.VMEM((seq, 256), BF16), pltpu.VMEM((seq, 256), BF16)],
        compiler_params=pltpu.CompilerParams(
            dimension_semantics=("parallel", "parallel", "arbitrary"), vmem_limit_bytes=VMEM_LIMIT_V7X),
        name="diff_prompt",
    )(q3, k3, v3, z3, gn3, *lams)
    return out.reshape(batch * seq, 1024)


ROWS_PG = PAGE_SIZE * 4
DEC1_PAGES_PER_CHUNK = 8
PAGES_PER_BLOCK = MOBA_BLOCK // PAGE_SIZE


def _pair_lo(x, even):
    return jnp.where(even, x, pltpu.roll(x, 1, axis=0))


def _pair_hi(x, even):
    return jnp.where(even, pltpu.roll(x, x.shape[0] - 1, axis=0), x)


def _dec1_kernel(pt_ref, qd_ref, qm_ref, kdn_ref, vdn_ref, zd_ref, gn_ref, lq1_ref, lk1_ref, lq2_ref, lk2_ref,
                 ck_hbm, cv_hbm, cmk_hbm, ad_ref, sel_ref, kbuf, vbuf, mbuf, sems, ks_sc,
                 *, n_pages, past_len):
    b = pl.program_id(0)
    n_b = pl.num_programs(0)
    ppc = DEC1_PAGES_PER_CHUNK
    n_chunks = n_pages // ppc
    rows = ppc * ROWS_PG
    streams = ((ck_hbm, kbuf), (cv_hbm, vbuf), (cmk_hbm, mbuf))

    def chunk_copies(bb, c, slot):
        out = []
        for k in range(ppc):
            page = pt_ref[bb, c * ppc + k]
            for si, (src, dst) in enumerate(streams):
                out.append(pltpu.make_async_copy(
                    src.at[page], dst.at[slot, pl.ds(k * ROWS_PG, ROWS_PG), :], sems.at[si, slot]))
        return out

    @pl.when(b == 0)
    def _():
        for cp in chunk_copies(0, 0, 0):
            cp.start()

    q8 = qd_ref[0]
    row = lax.broadcasted_iota(jnp.int32, (8, rows), 0)
    col = lax.broadcasted_iota(jnp.int32, (8, rows), 1)
    row1 = lax.broadcasted_iota(jnp.int32, (8, HEAD_DIM), 0)
    even = (row & 1) == 0
    even1 = (row1 & 1) == 0
    valid = (col & 3) == ((row & 1) * 2 + (row >> 2))
    slope = _pow2_neg(2 * ((row >> 1) + 1))
    posf = (col >> 2).astype(F32)

    def chunk_compute(c, slot, carry):
        m_old, l_old, acc = carry
        s = lax.dot_general(q8, kbuf[slot].astype(BF16), _NT, preferred_element_type=F32) * SCALE
        base = jnp.asarray(past_len - c * (ppc * PAGE_SIZE)).astype(F32)
        s = jnp.where(valid, s - slope * (base - posf), NEG_INF)
        m_new = jnp.maximum(m_old, jnp.max(s, axis=1, keepdims=True))
        alpha = jnp.exp(m_old - m_new)
        p = jnp.where(valid, jnp.exp(s - m_new[:, :1]), 0.0)
        l_new = alpha * l_old + jnp.sum(p, axis=1, keepdims=True)
        p_r = pltpu.roll(p, 2, axis=1)
        p_l = pltpu.roll(p, rows - 2, axis=1)
        w1 = jnp.where(even, p, pltpu.roll(p_r, 1, axis=0))
        w2 = jnp.where(even, pltpu.roll(p_l, 7, axis=0), p)
        w = jnp.concatenate([w1, w2], axis=0).astype(BF16)
        a16 = jnp.concatenate([_pair_lo(alpha, even1), _pair_hi(alpha, even1)], axis=0)
        acc = a16 * acc + jnp.dot(w, vbuf[slot].astype(BF16), preferred_element_type=F32)
        bpc = ppc // PAGES_PER_BLOCK
        ksum = jnp.sum(mbuf[slot].reshape(bpc, PAGES_PER_BLOCK * ROWS_PG // 8, 8, HEAD_DIM), axis=1)
        ks_sc[pl.ds(c * bpc, bpc)] = ksum
        return m_new, l_new, acc

    def pair_body(c2, carry):
        for u in range(2):
            c = 2 * c2 + u
            in_b = c + 1 < n_chunks

            @pl.when(in_b | (b + 1 < n_b))
            def _():
                for cp in chunk_copies(jnp.where(in_b, b, b + 1), jnp.where(in_b, c + 1, 0), 1 - u):
                    cp.start()

            for cp in chunk_copies(b, c, u):
                cp.wait()
            carry = chunk_compute(c, u, carry)
        return carry

    init = (jnp.full((8, HEAD_DIM), NEG_INF, F32), jnp.zeros((8, HEAD_DIM), F32), jnp.zeros((16, HEAD_DIM), F32))
    m_o, l_o, acc = lax.fori_loop(0, n_chunks // 2, pair_body, init)

    lam = _lambda_full(lq1_ref, lk1_ref, lq2_ref, lk2_ref)
    qf = q8.astype(F32)
    s_new = jnp.sum(qf * kdn_ref[0], axis=1, keepdims=True) * SCALE
    m_f = jnp.maximum(m_o, s_new)
    a_f = jnp.exp(m_o - m_f)
    p_new = jnp.exp(s_new - m_f)
    l_f = a_f * l_o + p_new
    vdn = vdn_ref[0]
    acc1 = _pair_lo(a_f, even1) * acc[:8] + _pair_lo(p_new, even1) * vdn
    acc2 = _pair_hi(a_f, even1) * acc[8:] + _pair_hi(p_new, even1) * vdn
    o = acc1 / _pair_lo(l_f, even1) - lam * (acc2 / _pair_hi(l_f, even1))
    ss = jnp.broadcast_to(jnp.sum(o * o, axis=1, keepdims=True), (8, HEAD_DIM))
    ss = ss + jnp.where(even1, pltpu.roll(ss, 7, axis=0), pltpu.roll(ss, 1, axis=0))
    o = o * lax.rsqrt(ss * (1.0 / DIFF_V_DIM) + RMS_EPS) * gn_ref[...] * (1.0 - LAM0)
    ad_ref[0] = (o * zd_ref[0].astype(F32)).astype(BF16)

    n_cols = ks_sc.shape[0] * 8
    ks = ks_sc[...].reshape(n_cols, HEAD_DIM)
    ks_hi = ks.astype(BF16)
    ks_lo = (ks - ks_hi.astype(F32)).astype(BF16)
    qm8 = qm_ref[0]
    g_all = (lax.dot_general(qm8, ks_hi, _NT, preferred_element_type=F32)
             + lax.dot_general(qm8, ks_lo, _NT, preferred_element_type=F32)) * (1.0 / MOBA_BLOCK)
    g2 = g_all + pltpu.roll(g_all, n_cols - 4, axis=1)
    rw = lax.broadcasted_iota(jnp.int32, (8, n_cols), 0)
    cl = lax.broadcasted_iota(jnp.int32, (8, n_cols), 1)
    clf = cl.astype(F32)
    gm = jnp.where((cl & 7) == (rw >> 1), g2, NEG_INF)
    lane = lax.broadcasted_iota(jnp.int32, (8, HEAD_DIM), 1)
    sel = jnp.zeros((8, HEAD_DIM), jnp.int32)
    for t in range(MOBA_TOPK):
        mx = jnp.max(gm, axis=1, keepdims=True)
        idx = jnp.min(jnp.where(gm == mx, clf, float(n_cols)), axis=1, keepdims=True)
        gm = jnp.where(clf == idx, NEG_INF, gm)
        sel = jnp.where(lane == t, idx.astype(jnp.int32) >> 3, sel)
    sel_ref[0] = sel


def _dec1(page_table, qd8, qm8, kdn8, vdn8, zd8, gn8, lams, ck3, cv3, cmk3, past_len):
    dec_batch, n_pages = page_table.shape
    assert n_pages % (2 * DEC1_PAGES_PER_CHUNK) == 0 and DEC1_PAGES_PER_CHUNK % PAGES_PER_BLOCK == 0
    n_blocks = n_pages // PAGES_PER_BLOCK
    rows = DEC1_PAGES_PER_CHUNK * ROWS_PG
    row8 = lambda: pl.BlockSpec((1, 8, HEAD_DIM), lambda b, pt: (b, 0, 0))
    vec = pl.BlockSpec((1, HEAD_DIM), lambda b, pt: (0, 0))
    hbm = lambda: pl.BlockSpec(memory_space=pl.ANY)
    grid_spec = pltpu.PrefetchScalarGridSpec(
        num_scalar_prefetch=1,
        grid=(dec_batch,),
        in_specs=[row8(), row8(), row8(), row8(), row8(),
                  pl.BlockSpec((8, HEAD_DIM), lambda b, pt: (0, 0)),
                  vec, vec, vec, vec,
                  hbm(), hbm(), hbm()],
        out_specs=(pl.BlockSpec((1, 8, HEAD_DIM), lambda b, pt: (b, 0, 0)),
                   pl.BlockSpec((1, 8, HEAD_DIM), lambda b, pt: (b, 0, 0))),
        scratch_shapes=[pltpu.VMEM((2, rows, HEAD_DIM), F32), pltpu.VMEM((2, rows, HEAD_DIM), F32),
                        pltpu.VMEM((2, rows, HEAD_DIM), F32), pltpu.SemaphoreType.DMA((3, 2)),
                        pltpu.VMEM((n_blocks, 8, HEAD_DIM), F32)],
    )
    return pl.pallas_call(
        functools.partial(_dec1_kernel, n_pages=n_pages, past_len=past_len),
        grid_spec=grid_spec,
        out_shape=(jax.ShapeDtypeStruct((dec_batch, 8, HEAD_DIM), BF16),
                   jax.ShapeDtypeStruct((dec_batch, 8, HEAD_DIM), jnp.int32)),
        compiler_params=pltpu.CompilerParams(
            dimension_semantics=("arbitrary",), vmem_limit_bytes=VMEM_LIMIT_V7X),
        name="dec1",
    )(page_table, qd8, qm8, kdn8, vdn8, zd8, gn8, *lams, ck3, cv3, cmk3)


def _dec2_kernel(pt_ref, sel_ref, qm_ref, kmn_ref, vmn_ref, zm_ref, ck_hbm, cv_hbm, am_ref, kbuf, vbuf, sems, o_sc,
                 *, past_len):
    g = pl.program_id(0)
    n_g = pl.num_programs(0)
    h = g % MOBA_HEADS
    slot = g % 2
    n_pg = MOBA_TOPK * PAGES_PER_BLOCK
    rows = n_pg * ROWS_PG

    def head_copies(gg, sl):
        bb = gg // MOBA_HEADS
        hh = gg % MOBA_HEADS
        out = []
        for t in range(MOBA_TOPK):
            blk = sel_ref[bb, hh * MOBA_TOPK + t]
            for f in range(PAGES_PER_BLOCK):
                page = pt_ref[bb, blk * PAGES_PER_BLOCK + f]
                dst = pl.ds((t * PAGES_PER_BLOCK + f) * ROWS_PG, ROWS_PG)
                out.append(pltpu.make_async_copy(ck_hbm.at[page], kbuf.at[sl, dst, :], sems.at[0, sl]))
                out.append(pltpu.make_async_copy(cv_hbm.at[page], vbuf.at[sl, dst, :], sems.at[1, sl]))
        return out

    @pl.when(g == 0)
    def _():
        for cp in head_copies(0, 0):
            cp.start()

    @pl.when(g + 1 < n_g)
    def _():
        for cp in head_copies(g + 1, 1 - slot):
            cp.start()

    for cp in head_copies(g, slot):
        cp.wait()

    q8 = qm_ref[0]
    row = lax.broadcasted_iota(jnp.int32, (8, rows), 0)
    col = lax.broadcasted_iota(jnp.int32, (8, rows), 1)
    row1 = lax.broadcasted_iota(jnp.int32, (8, HEAD_DIM), 0)
    b = g // MOBA_HEADS
    blk_col = jnp.zeros((8, rows), jnp.int32)
    for t in range(MOBA_TOPK):
        seg = (col >= t * PAGES_PER_BLOCK * ROWS_PG) & (col < (t + 1) * PAGES_PER_BLOCK * ROWS_PG)
        blk_col = jnp.where(seg, sel_ref[b, h * MOBA_TOPK + t], blk_col)
    kpos = blk_col * MOBA_BLOCK + ((col & (PAGES_PER_BLOCK * ROWS_PG - 1)) >> 2)
    s = lax.dot_general(q8, kbuf[slot].astype(BF16), _NT, preferred_element_type=F32) * SCALE
    s = s - _pow2_neg(row + 1) * (past_len - kpos).astype(F32)
    valid = (row == h) & ((col & 3) == (h >> 1))
    s = jnp.where(valid, s, NEG_INF)
    s_new = jnp.sum(q8.astype(F32) * kmn_ref[0], axis=1, keepdims=True) * SCALE
    m = jnp.maximum(jnp.max(s, axis=1, keepdims=True), s_new)
    p = jnp.where(valid, jnp.exp(s - m), 0.0)
    p_new = jnp.exp(s_new - m)
    l = jnp.sum(p, axis=1, keepdims=True) + p_new
    o = (jnp.dot(p.astype(BF16), vbuf[slot].astype(BF16), preferred_element_type=F32) + p_new * vmn_ref[0]) / l

    @pl.when(h == 0)
    def _():
        o_sc[...] = jnp.zeros(o_sc.shape, F32)

    o_sc[...] = jnp.where(row1 == h, o, o_sc[...])

    @pl.when(h == MOBA_HEADS - 1)
    def _():
        am_ref[0] = (o_sc[...] * zm_ref[0].astype(F32)).astype(BF16)


def _dec2(page_table, sel, qm8, kmn8, vmn8, zm8, cmk3, cmv3, past_len):
    dec_batch = page_table.shape[0]
    rows = MOBA_TOPK * PAGES_PER_BLOCK * ROWS_PG
    row8 = lambda: pl.BlockSpec((1, 8, HEAD_DIM), lambda g, pt, sl: (g // MOBA_HEADS, 0, 0))
    hbm = lambda: pl.BlockSpec(memory_space=pl.ANY)
    grid_spec = pltpu.PrefetchScalarGridSpec(
        num_scalar_prefetch=2,
        grid=(dec_batch * MOBA_HEADS,),
        in_specs=[row8(), row8(), row8(), row8(), hbm(), hbm()],
        out_specs=pl.BlockSpec((1, 8, HEAD_DIM), lambda g, pt, sl: (g // MOBA_HEADS, 0, 0)),
        scratch_shapes=[pltpu.VMEM((2, rows, HEAD_DIM), F32), pltpu.VMEM((2, rows, HEAD_DIM), F32),
                        pltpu.SemaphoreType.DMA((2, 2)), pltpu.VMEM((8, HEAD_DIM), F32)],
    )
    return pl.pallas_call(
        functools.partial(_dec2_kernel, past_len=past_len),
        grid_spec=grid_spec,
        out_shape=jax.ShapeDtypeStruct((dec_batch, 8, HEAD_DIM), BF16),
        compiler_params=pltpu.CompilerParams(
            dimension_semantics=("arbitrary",), vmem_limit_bytes=VMEM_LIMIT_V7X),
        name="dec2",
    )(page_table, sel, qm8, kmn8, vmn8, zm8, cmk3, cmv3)


def _post_kernel(am_ref, ad_ref, g_ref, x_ref, wmo_ref, wdo_ref, wo_ref, lng_ref, lnb_ref, y_ref):
    ym = jnp.dot(am_ref[...], wmo_ref[...], preferred_element_type=F32)
    yd = jnp.dot(ad_ref[...], wdo_ref[...], preferred_element_type=F32)
    g = g_ref[...].astype(F32)
    merged = g[:, :D_MODEL] * ym + g[:, D_MODEL:] * yd
    f = jnp.dot(merged.astype(BF16), wo_ref[...], preferred_element_type=F32)
    u = ALPHA * x_ref[...] + f
    mu = jnp.mean(u, axis=1, keepdims=True)
    d = u - mu
    var = jnp.mean(d * d, axis=1, keepdims=True)
    y_ref[...] = d * lax.rsqrt(var + LN_EPS) * lng_ref[...] + lnb_ref[...]


def _post(am, ad, gt, x2d, wmo, wdo, wo, lng, lnb, tm):
    m_rows = x2d.shape[0]
    rowblk = lambda w: pl.BlockSpec((tm, w), lambda m: (m, 0))
    full = lambda a: pl.BlockSpec(a.shape, lambda m: (0, 0))
    return pl.pallas_call(
        _post_kernel,
        grid=(m_rows // tm,),
        in_specs=[rowblk(1024), rowblk(1024), rowblk(N_BRANCH * D_MODEL), rowblk(D_MODEL),
                  full(wmo), full(wdo), full(wo), full(lng), full(lnb)],
        out_specs=rowblk(D_MODEL),
        out_shape=jax.ShapeDtypeStruct((m_rows, D_MODEL), F32),
        compiler_params=pltpu.CompilerParams(
            dimension_semantics=("parallel",), vmem_limit_bytes=VMEM_LIMIT_V7X),
        name="post",
    )(am, ad, gt, x2d, wmo, wdo, wo, lng, lnb)


def kernel(x_prompt, x_sample, cache_moba_k, cache_moba_v, cache_diff_k, cache_diff_v, page_table, w_in, w_moba_o, w_diff_o, w_out, lambda_q1, lambda_k1, lambda_q2, lambda_k2, diff_norm_g, ln_g, ln_b):
    assert w_in.shape[0] == DEPTH == 1
    batch, seq, _ = x_prompt.shape
    dec_batch, dec_seq, _ = x_sample.shape
    assert dec_seq == 1 and seq % MOBA_BLOCK == 0
    n_pages = page_table.shape[1]
    past_len = n_pages * PAGE_SIZE
    assert past_len % MOBA_BLOCK == 0 and past_len // MOBA_BLOCK >= MOBA_TOPK
    n_phys = cache_moba_k.shape[1]

    w_bf = w_in[0].astype(BF16)
    wmo = w_moba_o[0].astype(BF16)
    wdo = w_diff_o[0].astype(BF16)
    wo = w_out[0].astype(BF16)
    lams = (lambda_q1, lambda_k1, lambda_q2, lambda_k2)
    lng = ln_g.reshape(1, D_MODEL)
    lnb = ln_b.reshape(1, D_MODEL)

    xp = x_prompt.reshape(batch * seq, D_MODEL)
    qm, km, vm, zm, qd, kd, vd, zd, gt = _proj(xp, w_bf, tm=512)
    am = _moba_prompt(qm, km, vm, zm, batch, seq)
    ad = _diff_prompt(qd, kd, vd, zd, diff_norm_g.reshape(DIFF_KV_HEADS, 2, DIFF_V_DIM), lams, batch, seq)
    y_p = _post(am, ad, gt, xp, wmo, wdo, wo, lng, lnb, tm=256)

    xs = x_sample.reshape(dec_batch, D_MODEL)
    qm_s, km_s, vm_s, zm_s, qd_s, kd_s, vd_s, zd_s, gt_s = _proj(xs, w_bf, tm=dec_batch)
    cmk3 = cache_moba_k.reshape(n_phys, ROWS_PG, HEAD_DIM)
    cmv3 = cache_moba_v.reshape(n_phys, ROWS_PG, HEAD_DIM)

    def diff_pages(c):
        c = c.reshape(n_phys, PAGE_SIZE, DIFF_KV_HEADS, 2, HEAD_DIM)
        return jnp.transpose(c, (0, 1, 3, 2, 4)).reshape(n_phys, ROWS_PG, HEAD_DIM)

    cdk3 = diff_pages(cache_diff_k)
    cdv3 = diff_pages(cache_diff_v)

    def pair_rows(a):
        a = a.reshape(dec_batch, DIFF_KV_HEADS, 1, 2, HEAD_DIM)
        return jnp.broadcast_to(a, (dec_batch, DIFF_KV_HEADS, 2, 2, HEAD_DIM)).reshape(dec_batch, 8, HEAD_DIM)

    qd8 = qd_s.reshape(dec_batch, 8, HEAD_DIM)
    qm8 = qm_s.reshape(dec_batch, 8, HEAD_DIM)
    zd8 = zd_s.reshape(dec_batch, 8, HEAD_DIM)
    zm8 = zm_s.reshape(dec_batch, 8, HEAD_DIM)
    gn8 = diff_norm_g.reshape(8, HEAD_DIM)
    ad_s, sel = _dec1(page_table, qd8, qm8, pair_rows(kd_s), pair_rows(vd_s), zd8, gn8, lams,
                      cdk3, cdv3, cmk3, past_len)
    sel2 = sel[:, :, :MOBA_TOPK].reshape(dec_batch, MOBA_HEADS * MOBA_TOPK)
    kmn8 = jnp.repeat(km_s.reshape(dec_batch, MOBA_KV_HEADS, HEAD_DIM), 2, axis=1)
    vmn8 = jnp.repeat(vm_s.reshape(dec_batch, MOBA_KV_HEADS, HEAD_DIM), 2, axis=1)
    am_s = _dec2(page_table, sel2, qm8, kmn8, vmn8, zm8, cmk3, cmv3, past_len)
    y_s = _post(am_s.reshape(dec_batch, 1024), ad_s.reshape(dec_batch, 1024), gt_s, xs,
                wmo, wdo, wo, lng, lnb, tm=dec_batch)

    return (y_p.reshape(batch, seq, D_MODEL), y_s.reshape(dec_batch, 1, D_MODEL),
            km.reshape(DEPTH, batch, seq, MOBA_KV_HEADS, HEAD_DIM),
            vm.reshape(DEPTH, batch, seq, MOBA_KV_HEADS, HEAD_DIM),
            kd.reshape(DEPTH, batch, seq, DIFF_KV_HEADS, 2 * HEAD_DIM),
            vd.reshape(DEPTH, batch, seq, DIFF_KV_HEADS, DIFF_V_DIM),
            km_s.reshape(DEPTH, dec_batch, 1, MOBA_KV_HEADS, HEAD_DIM),
            vm_s.reshape(DEPTH, dec_batch, 1, MOBA_KV_HEADS, HEAD_DIM),
            kd_s.reshape(DEPTH, dec_batch, 1, DIFF_KV_HEADS, 2 * HEAD_DIM),
            vd_s.reshape(DEPTH, dec_batch, 1, DIFF_KV_HEADS, DIFF_V_DIM))
```

```python
import functools
import math

import jax
import jax.numpy as jnp
from jax import lax
from jax.experimental import pallas as pl
from jax.experimental.pallas import tpu as pltpu

F32 = jnp.float32
BF16 = jnp.bfloat16

D_MODEL = 2048
HEAD_DIM = 128
PAGE_SIZE = 128
MOBA_HEADS = 8
MOBA_KV_HEADS = 4
MOBA_BLOCK = 256
MOBA_TOPK = 3
DIFF_HEADS = 4
DIFF_KV_HEADS = 2
DIFF_V_DIM = 2 * HEAD_DIM
N_BRANCH = 2
DEPTH = 1
ALPHA = (2 * DEPTH) ** 0.25
LN_EPS = 1e-5
RMS_EPS = 1e-5
NEG_INF = -1e30
SCALE = HEAD_DIM ** -0.5
LAM0 = 0.8 - 0.6 * math.exp(-0.3 * 0)

VMEM_LIMIT_V7X = 56 * 1024 * 1024

PROJ_TN = 1024

_NT = (((1,), (1,)), ((), ()))


def _pow2_neg(k):
    return lax.bitcast_convert_type((127 - k) << 23, F32)


def _silu(v):
    return v * (1.0 / (1.0 + jnp.exp(-v)))


def _sigmoid(v):
    return 1.0 / (1.0 + jnp.exp(-v))


def _proj_kernel(x_ref, w_ref, wqt_ref, qmt_ref, km_ref, vm_ref, zm_ref, qdt_ref, kd_ref, vd_ref, zd_ref, gt_ref,
                 xb_ref):
    n = pl.program_id(1)

    @pl.when(n == 0)
    def _():
        xb_ref[...] = x_ref[...].astype(BF16)

    def tile():
        return jnp.dot(xb_ref[...], w_ref[...], preferred_element_type=F32)

    def tile_t():
        return lax.dot_general(wqt_ref[...], xb_ref[...], _NT, preferred_element_type=F32)

    half = PROJ_TN // 2

    @pl.when(n == 0)
    def _():
        qmt_ref[...] = tile_t().astype(BF16)

    @pl.when(n == 1)
    def _():
        acc = tile()
        km_ref[...] = acc[:, :half]
        vm_ref[...] = acc[:, half:]

    @pl.when(n == 2)
    def _():
        zm_ref[...] = _silu(tile()).astype(BF16)

    @pl.when(n == 3)
    def _():
        qdt_ref[...] = tile_t().astype(BF16)

    @pl.when(n == 4)
    def _():
        acc = tile()
        kd_ref[...] = acc[:, :half]
        vd_ref[...] = acc[:, half:]

    @pl.when(n == 5)
    def _():
        zd_ref[...] = _silu(tile()).astype(BF16)

    @pl.when(n >= 6)
    def _():
        gt_ref[...] = _sigmoid(tile()).astype(BF16)


def _proj(x2d, w_bf, wqt_bf, tm):
    m_rows = x2d.shape[0]
    n_tiles = w_bf.shape[1] // PROJ_TN

    def out_map(lo, hi):
        return lambda m, n: (m, jnp.clip(n - lo, 0, hi - lo - 1))

    half = PROJ_TN // 2
    out_shape = (
        jax.ShapeDtypeStruct((1024, m_rows), BF16),
        jax.ShapeDtypeStruct((m_rows, 512), F32),
        jax.ShapeDtypeStruct((m_rows, 512), F32),
        jax.ShapeDtypeStruct((m_rows, 1024), BF16),
        jax.ShapeDtypeStruct((1024, m_rows), BF16),
        jax.ShapeDtypeStruct((m_rows, 512), F32),
        jax.ShapeDtypeStruct((m_rows, 512), F32),
        jax.ShapeDtypeStruct((m_rows, 1024), BF16),
        jax.ShapeDtypeStruct((m_rows, 4096), BF16),
    )
    qt_spec = lambda: pl.BlockSpec((PROJ_TN, tm), lambda m, n: (0, m))
    out_specs = (
        qt_spec(),
        pl.BlockSpec((tm, half), out_map(1, 2)),
        pl.BlockSpec((tm, half), out_map(1, 2)),
        pl.BlockSpec((tm, PROJ_TN), out_map(2, 3)),
        qt_spec(),
        pl.BlockSpec((tm, half), out_map(4, 5)),
        pl.BlockSpec((tm, half), out_map(4, 5)),
        pl.BlockSpec((tm, PROJ_TN), out_map(5, 6)),
        pl.BlockSpec((tm, PROJ_TN), out_map(6, 10)),
    )
    w_map = lambda m, n: (0, jnp.where(n == 0, 1, jnp.where(n == 3, 4, n)))
    return pl.pallas_call(
        _proj_kernel,
        grid=(m_rows // tm, n_tiles),
        in_specs=[pl.BlockSpec((tm, D_MODEL), lambda m, n: (m, 0)),
                  pl.BlockSpec((D_MODEL, PROJ_TN), w_map),
                  pl.BlockSpec((PROJ_TN, D_MODEL), lambda m, n: (jnp.where(n < 3, 0, 1), 0))],
        out_specs=out_specs,
        out_shape=out_shape,
        scratch_shapes=[pltpu.VMEM((tm, D_MODEL), BF16)],
        compiler_params=pltpu.CompilerParams(
            dimension_semantics=("parallel", "arbitrary"), vmem_limit_bytes=VMEM_LIMIT_V7X),
        name="proj",
    )(x2d, w_bf, wqt_bf)


LOG2E = 1.4426950408889634
SCALE_L2 = SCALE * LOG2E
MASKED = 1e30


def _alibi_tiles_t(blk):
    c_i = lax.broadcasted_iota(jnp.int32, (blk, 2 * blk), 0)
    q_i = lax.broadcasted_iota(jnp.int32, (blk, 2 * blk), 1)
    g = q_i // blk
    rel = (q_i - g * blk - c_i).astype(F32)
    return rel, g[:1], rel >= 0.0


def _slope_row_l2(g_row, head0, slope_mul):
    return _pow2_neg(slope_mul * (head0 + g_row + 1)) * LOG2E


def _moba_prompt_kernel(qt_ref, k_ref, v_ref, z_ref, o_ref, kb_sc, vt_sc, km_sc, bias_sc, acc_sc, *, n_blk):
    j = pl.program_id(1)
    blk = MOBA_BLOCK
    n_kv = MOBA_KV_HEADS
    rel, g_row, causal = _alibi_tiles_t(blk)

    @pl.when(j == 0)
    def _():
        for hk in range(n_kv):
            k = k_ref[0, :, hk * HEAD_DIM:(hk + 1) * HEAD_DIM].reshape(n_blk, blk, HEAD_DIM)
            kb_sc[hk] = k.astype(BF16)
            km_sc[hk] = jnp.sum(k, axis=1) * (1.0 / blk)
            for n in range(n_blk):
                vt_sc[hk, n] = v_ref[0, n * blk:(n + 1) * blk, hk * HEAD_DIM:(hk + 1) * HEAD_DIM].T.astype(BF16)
            bias_sc[hk] = rel * _slope_row_l2(g_row, 2 * hk, 1)

    qt2s, ranks, slope_blks, init = [], [], [], []
    for hk in range(n_kv):
        qt = qt_ref[hk * 2 * HEAD_DIM:(hk + 1) * 2 * HEAD_DIM, :]
        qt2 = jnp.concatenate([qt[:HEAD_DIM], qt[HEAD_DIM:]], axis=1)
        kmean = km_sc[hk]
        kmean_hi = kmean.astype(BF16)
        kmean_lo = (kmean - kmean_hi.astype(F32)).astype(BF16)
        gate = (jnp.dot(kmean_hi, qt2, preferred_element_type=F32)
                + jnp.dot(kmean_lo, qt2, preferred_element_type=F32))
        sub = lax.broadcasted_iota(jnp.int32, gate.shape, 0)
        gate = jnp.where(sub < j, gate, NEG_INF)
        rank = jnp.zeros(gate.shape, F32)
        for m in range(n_blk):
            gm = gate[m:m + 1, :]
            ge = jnp.where(gm >= gate, 1.0, 0.0)
            gt = jnp.where(gm > gate, 1.0, 0.0)
            rank = rank + jnp.where(sub > m, ge, gt)
        u = jnp.dot(kb_sc[hk, j], qt2, preferred_element_type=F32) * SCALE_L2 - bias_sc[hk]
        u = jnp.where(causal, u, -MASKED)
        m0 = jnp.max(u, axis=0, keepdims=True)
        p = jnp.exp2(u - m0)
        acc_sc[hk] = jnp.dot(vt_sc[hk, j], p.astype(BF16), preferred_element_type=F32)
        qt2s.append(qt2)
        ranks.append(rank)
        slope_blks.append(_slope_row_l2(g_row, 2 * hk, 1) * float(blk))
        init += [m0, jnp.sum(p, axis=0, keepdims=True)]

    def body(n, carry):
        dist_blocks = (j - n).astype(F32)
        out = []
        for hk in range(n_kv):
            m_run, l_run = carry[2 * hk:2 * hk + 2]
            u = jnp.dot(kb_sc[hk, n], qt2s[hk], preferred_element_type=F32) * SCALE_L2 - bias_sc[hk]
            sub = lax.broadcasted_iota(jnp.int32, ranks[hk].shape, 0)
            rk = jnp.sum(jnp.where(sub == n, ranks[hk], 0.0), axis=0, keepdims=True)
            rowc = jnp.where(rk < float(MOBA_TOPK), slope_blks[hk] * dist_blocks, MASKED)
            m_new = jnp.maximum(m_run, jnp.max(u, axis=0, keepdims=True) - rowc)
            alpha = jnp.exp2(m_run - m_new)
            p = jnp.exp2(u - (rowc + m_new))
            out += [m_new, alpha * l_run + jnp.sum(p, axis=0, keepdims=True)]
            acc_sc[hk] = alpha * acc_sc[hk] + jnp.dot(vt_sc[hk, n], p.astype(BF16), preferred_element_type=F32)
        return tuple(out)

    fin = lax.fori_loop(0, j, body, tuple(init))
    for hk in range(n_kv):
        ot = acc_sc[hk] / fin[2 * hk + 1]
        o2 = jnp.concatenate([ot[:, :blk].T, ot[:, blk:].T], axis=1)
        cols = slice(hk * 2 * HEAD_DIM, (hk + 1) * 2 * HEAD_DIM)
        o_ref[0, :, cols] = (o2 * z_ref[0, :, cols].astype(F32)).astype(BF16)


def _moba_prompt(qmt, km, vm, zm, batch, seq):
    n_blk = seq // MOBA_BLOCK
    blk = MOBA_BLOCK
    width = MOBA_HEADS * HEAD_DIM
    kv_width = MOBA_KV_HEADS * HEAD_DIM
    k3 = km.reshape(batch, seq, kv_width)
    v3 = vm.reshape(batch, seq, kv_width)
    z3 = zm.reshape(batch, seq, width)
    out = pl.pallas_call(
        functools.partial(_moba_prompt_kernel, n_blk=n_blk),
        grid=(batch, n_blk),
        in_specs=[pl.BlockSpec((width, blk), lambda b, j: (0, b * n_blk + j)),
                  pl.BlockSpec((1, seq, kv_width), lambda b, j: (b, 0, 0)),
                  pl.BlockSpec((1, seq, kv_width), lambda b, j: (b, 0, 0)),
                  pl.BlockSpec((1, blk, width), lambda b, j: (b, j, 0))],
        out_specs=pl.BlockSpec((1, blk, width), lambda b, j: (b, j, 0)),
        out_shape=jax.ShapeDtypeStruct((batch, seq, width), BF16),
        scratch_shapes=[pltpu.VMEM((MOBA_KV_HEADS, n_blk, blk, HEAD_DIM), BF16),
                        pltpu.VMEM((MOBA_KV_HEADS, n_blk, HEAD_DIM, blk), BF16),
                        pltpu.VMEM((MOBA_KV_HEADS, n_blk, HEAD_DIM), F32),
                        pltpu.VMEM((MOBA_KV_HEADS, blk, 2 * blk), F32),
                        pltpu.VMEM((MOBA_KV_HEADS, HEAD_DIM, 2 * blk), F32)],
        compiler_params=pltpu.CompilerParams(
            dimension_semantics=("parallel", "arbitrary"), vmem_limit_bytes=VMEM_LIMIT_V7X),
        name="moba_prompt",
    )(qmt, k3, v3, z3)
    return out.reshape(batch * seq, width)


def _lambda_full(lq1_ref, lk1_ref, lq2_ref, lk2_ref):
    a = jnp.sum(lq1_ref[...] * lk1_ref[...], axis=1, keepdims=True)
    b = jnp.sum(lq2_ref[...] * lk2_ref[...], axis=1, keepdims=True)
    return jnp.exp(a) - jnp.exp(b) + LAM0


def _diff_prompt_kernel(qt_ref, k_ref, v_ref, z_ref, gn_ref, lq1_ref, lk1_ref, lq2_ref, lk2_ref, o_ref,
                        kb_sc, vt_sc, bias_sc, acc_sc, *, n_blk):
    j = pl.program_id(1)
    blk = MOBA_BLOCK
    n_kv = DIFF_KV_HEADS
    rel, g_row, causal = _alibi_tiles_t(blk)

    @pl.when(j == 0)
    def _():
        for hk in range(n_kv):
            kb_sc[hk] = k_ref[0, :, hk * 256:(hk + 1) * 256].reshape(n_blk, blk, 2 * HEAD_DIM).astype(BF16)
            for n in range(n_blk):
                vt_sc[hk, n] = v_ref[0, n * blk:(n + 1) * blk, hk * 256:(hk + 1) * 256].T.astype(BF16)
            bias_sc[hk] = rel * _slope_row_l2(g_row, 2 * hk, 2)

    lam = _lambda_full(lq1_ref, lk1_ref, lq2_ref, lk2_ref)
    chains = [(hk, mi) for hk in range(n_kv) for mi in range(2)]
    qts = {}
    for hk in range(n_kv):
        qt = qt_ref[hk * 4 * HEAD_DIM:(hk + 1) * 4 * HEAD_DIM, :]
        qts[hk, 0] = jnp.concatenate([qt[0:128], qt[256:384]], axis=1)
        qts[hk, 1] = jnp.concatenate([qt[128:256], qt[384:512]], axis=1)
    slope_blk = [_slope_row_l2(g_row, 2 * hk, 2) * float(blk) for hk in range(n_kv)]

    def scores(hk, mi, n):
        kn = kb_sc[hk, n, :, mi * HEAD_DIM:(mi + 1) * HEAD_DIM]
        return jnp.dot(kn, qts[hk, mi], preferred_element_type=F32) * SCALE_L2 - bias_sc[hk]

    init = []
    for ci, (hk, mi) in enumerate(chains):
        u = jnp.where(causal, scores(hk, mi, j), -MASKED)
        m0 = jnp.max(u, axis=0, keepdims=True)
        p = jnp.exp2(u - m0)
        init += [m0, jnp.sum(p, axis=0, keepdims=True)]
        acc_sc[ci] = jnp.dot(vt_sc[hk, j], p.astype(BF16), preferred_element_type=F32)

    def body(n, carry):
        dist_blocks = (j - n).astype(F32)
        out = []
        for ci, (hk, mi) in enumerate(chains):
            m_run, l_run = carry[2 * ci:2 * ci + 2]
            rowc = slope_blk[hk] * dist_blocks
            u = scores(hk, mi, n)
            m_new = jnp.maximum(m_run, jnp.max(u, axis=0, keepdims=True) - rowc)
            alpha = jnp.exp2(m_run - m_new)
            p = jnp.exp2(u - (rowc + m_new))
            out += [m_new, alpha * l_run + jnp.sum(p, axis=0, keepdims=True)]
            acc_sc[ci] = alpha * acc_sc[ci] + jnp.dot(vt_sc[hk, n], p.astype(BF16), preferred_element_type=F32)
        return tuple(out)

    fin = lax.fori_loop(0, j, body, tuple(init))
    for hk in range(n_kv):
        c1, c2 = 2 * hk, 2 * hk + 1
        ot = acc_sc[c1] / fin[2 * c1 + 1] - lam * (acc_sc[c2] / fin[2 * c2 + 1])
        ms = jnp.mean(ot * ot, axis=0, keepdims=True)
        ot = ot * lax.rsqrt(ms + RMS_EPS)
        gn = gn_ref[hk]
        o2 = jnp.concatenate([ot[:, :blk].T * gn[0:1, :], ot[:, blk:].T * gn[1:2, :]], axis=1) * (1.0 - LAM0)
        cols = slice(hk * 2 * DIFF_V_DIM, (hk + 1) * 2 * DIFF_V_DIM)
        o_ref[0, :, cols] = (o2 * z_ref[0, :, cols].astype(F32)).astype(BF16)


def _diff_prompt(qdt, kd, vd, zd, gn3, lams, batch, seq):
    n_blk = seq // MOBA_BLOCK
    blk = MOBA_BLOCK
    width = DIFF_HEADS * DIFF_V_DIM
    kv_width = DIFF_KV_HEADS * DIFF_V_DIM
    k3 = kd.reshape(batch, seq, kv_width)
    v3 = vd.reshape(batch, seq, kv_width)
    z3 = zd.reshape(batch, seq, width)
    vec = pl.BlockSpec((1, HEAD_DIM), lambda b, j: (0, 0))
    out = pl.pallas_call(
        functools.partial(_diff_prompt_kernel, n_blk=n_blk),
        grid=(batch, n_blk),
        in_specs=[pl.BlockSpec((width, blk), lambda b, j: (0, b * n_blk + j)),
                  pl.BlockSpec((1, seq, kv_width), lambda b, j: (b, 0, 0)),
                  pl.BlockSpec((1, seq, kv_width), lambda b, j: (b, 0, 0)),
                  pl.BlockSpec((1, blk, width), lambda b, j: (b, j, 0)),
                  pl.BlockSpec((DIFF_KV_HEADS, 2, DIFF_V_DIM), lambda b, j: (0, 0, 0)),
                  vec, vec, vec, vec],
        out_specs=pl.BlockSpec((1, blk, width), lambda b, j: (b, j, 0)),
        out_shape=jax.ShapeDtypeStruct((batch, seq, width), BF16),
        scratch_shapes=[pltpu.VMEM((DIFF_KV_HEADS, n_blk, blk, 2 * HEAD_DIM), BF16),
                        pltpu.VMEM((DIFF_KV_HEADS, n_blk, DIFF_V_DIM, blk), BF16),
                        pltpu.VMEM((DIFF_KV_HEADS, blk, 2 * blk), F32),
                        pltpu.VMEM((2 * DIFF_KV_HEADS, DIFF_V_DIM, 2 * blk), F32)],
        compiler_params=pltpu.CompilerParams(
            dimension_semantics=("parallel", "arbitrary"), vmem_limit_bytes=VMEM_LIMIT_V7X),
        name="diff_prompt",
    )(qdt, k3, v3, z3, gn3, *lams)
    return out.reshape(batch * seq, width)


ROWS_PG = PAGE_SIZE * 4
DEC1_PAGES_PER_CHUNK = 8
PAGES_PER_BLOCK = MOBA_BLOCK // PAGE_SIZE


def _pair_lo(x, even):
    return jnp.where(even, x, pltpu.roll(x, 1, axis=0))


def _pair_hi(x, even):
    return jnp.where(even, pltpu.roll(x, x.shape[0] - 1, axis=0), x)


def _dec1_kernel(pt_ref, qd_ref, qm_ref, kdn_ref, vdn_ref, zd_ref, gn_ref, lq1_ref, lk1_ref, lq2_ref, lk2_ref,
                 ck_hbm, cv_hbm, cmk_hbm, ad_ref, sel_ref, kbuf, vbuf, mbuf, sems, ks_sc,
                 *, n_pages, past_len):
    b = pl.program_id(0)
    n_b = pl.num_programs(0)
    ppc = DEC1_PAGES_PER_CHUNK
    n_chunks = n_pages // ppc
    rows = ppc * ROWS_PG
    streams = ((ck_hbm, kbuf), (cv_hbm, vbuf), (cmk_hbm, mbuf))

    def chunk_copies(bb, c, slot):
        out = []
        for k in range(ppc):
            page = pt_ref[bb, c * ppc + k]
            for si, (src, dst) in enumerate(streams):
                out.append(pltpu.make_async_copy(
                    src.at[page], dst.at[slot, pl.ds(k * ROWS_PG, ROWS_PG), :], sems.at[si, slot]))
        return out

    @pl.when(b == 0)
    def _():
        for cp in chunk_copies(0, 0, 0):
            cp.start()

    q8 = qd_ref[0]
    row = lax.broadcasted_iota(jnp.int32, (8, rows), 0)
    col = lax.broadcasted_iota(jnp.int32, (8, rows), 1)
    row1 = lax.broadcasted_iota(jnp.int32, (8, HEAD_DIM), 0)
    even = (row & 1) == 0
    even1 = (row1 & 1) == 0
    valid = (col & 3) == ((row & 1) * 2 + (row >> 2))
    slope = _pow2_neg(2 * ((row >> 1) + 1))
    posf = (col >> 2).astype(F32)

    def chunk_compute(c, slot, carry):
        m_old, l_old, acc = carry
        s = lax.dot_general(q8, kbuf[slot].astype(BF16), _NT, preferred_element_type=F32) * SCALE
        base = jnp.asarray(past_len - c * (ppc * PAGE_SIZE)).astype(F32)
        s = jnp.where(valid, s - slope * (base - posf), NEG_INF)
        m_new = jnp.maximum(m_old, jnp.max(s, axis=1, keepdims=True))
        alpha = jnp.exp(m_old - m_new)
        p = jnp.where(valid, jnp.exp(s - m_new[:, :1]), 0.0)
        l_new = alpha * l_old + jnp.sum(p, axis=1, keepdims=True)
        p_r = pltpu.roll(p, 2, axis=1)
        p_l = pltpu.roll(p, rows - 2, axis=1)
        w1 = jnp.where(even, p, pltpu.roll(p_r, 1, axis=0))
        w2 = jnp.where(even, pltpu.roll(p_l, 7, axis=0), p)
        w = jnp.concatenate([w1, w2], axis=0).astype(BF16)
        a16 = jnp.concatenate([_pair_lo(alpha, even1), _pair_hi(alpha, even1)], axis=0)
        acc = a16 * acc + jnp.dot(w, vbuf[slot].astype(BF16), preferred_element_type=F32)
        bpc = ppc // PAGES_PER_BLOCK
        ksum = jnp.sum(mbuf[slot].reshape(bpc, PAGES_PER_BLOCK * ROWS_PG // 8, 8, HEAD_DIM), axis=1)
        ks_sc[pl.ds(c * bpc, bpc)] = ksum
        return m_new, l_new, acc

    def pair_body(c2, carry):
        for u in range(2):
            c = 2 * c2 + u
            in_b = c + 1 < n_chunks

            @pl.when(in_b | (b + 1 < n_b))
            def _():
                for cp in chunk_copies(jnp.where(in_b, b, b + 1), jnp.where(in_b, c + 1, 0), 1 - u):
                    cp.start()

            for cp in chunk_copies(b, c, u):
                cp.wait()
            carry = chunk_compute(c, u, carry)
        return carry

    init = (jnp.full((8, HEAD_DIM), NEG_INF, F32), jnp.zeros((8, HEAD_DIM), F32), jnp.zeros((16, HEAD_DIM), F32))
    m_o, l_o, acc = lax.fori_loop(0, n_chunks // 2, pair_body, init)

    lam = _lambda_full(lq1_ref, lk1_ref, lq2_ref, lk2_ref)
    qf = q8.astype(F32)
    s_new = jnp.sum(qf * kdn_ref[0], axis=1, keepdims=True) * SCALE
    m_f = jnp.maximum(m_o, s_new)
    a_f = jnp.exp(m_o - m_f)
    p_new = jnp.exp(s_new - m_f)
    l_f = a_f * l_o + p_new
    vdn = vdn_ref[0]
    acc1 = _pair_lo(a_f, even1) * acc[:8] + _pair_lo(p_new, even1) * vdn
    acc2 = _pair_hi(a_f, even1) * acc[8:] + _pair_hi(p_new, even1) * vdn
    o = acc1 / _pair_lo(l_f, even1) - lam * (acc2 / _pair_hi(l_f, even1))
    ss = jnp.broadcast_to(jnp.sum(o * o, axis=1, keepdims=True), (8, HEAD_DIM))
    ss = ss + jnp.where(even1, pltpu.roll(ss, 7, axis=0), pltpu.roll(ss, 1, axis=0))
    o = o * lax.rsqrt(ss * (1.0 / DIFF_V_DIM) + RMS_EPS) * gn_ref[...] * (1.0 - LAM0)
    ad_ref[0] = (o * zd_ref[0].astype(F32)).astype(BF16)

    n_cols = ks_sc.shape[0] * 8
    ks = ks_sc[...].reshape(n_cols, HEAD_DIM)
    ks_hi = ks.astype(BF16)
    ks_lo = (ks - ks_hi.astype(F32)).astype(BF16)
    qm8 = qm_ref[0]
    g_all = (lax.dot_general(qm8, ks_hi, _NT, preferred_element_type=F32)
             + lax.dot_general(qm8, ks_lo, _NT, preferred_element_type=F32)) * (1.0 / MOBA_BLOCK)
    g2 = g_all + pltpu.roll(g_all, n_cols - 4, axis=1)
    rw = lax.broadcasted_iota(jnp.int32, (8, n_cols), 0)
    cl = lax.broadcasted_iota(jnp.int32, (8, n_cols), 1)
    clf = cl.astype(F32)
    gm = jnp.where((cl & 7) == (rw >> 1), g2, NEG_INF)
    lane = lax.broadcasted_iota(jnp.int32, (8, HEAD_DIM), 1)
    sel = jnp.zeros((8, HEAD_DIM), jnp.int32)
    for t in range(MOBA_TOPK):
        mx = jnp.max(gm, axis=1, keepdims=True)
        idx = jnp.min(jnp.where(gm == mx, clf, float(n_cols)), axis=1, keepdims=True)
        gm = jnp.where(clf == idx, NEG_INF, gm)
        sel = jnp.where(lane == t, idx.astype(jnp.int32) >> 3, sel)
    sel_ref[0] = sel


def _dec1(page_table, qd8, qm8, kdn8, vdn8, zd8, gn8, lams, ck3, cv3, cmk3, past_len):
    dec_batch, n_pages = page_table.shape
    assert n_pages % (2 * DEC1_PAGES_PER_CHUNK) == 0 and DEC1_PAGES_PER_CHUNK % PAGES_PER_BLOCK == 0
    n_blocks = n_pages // PAGES_PER_BLOCK
    rows = DEC1_PAGES_PER_CHUNK * ROWS_PG
    row8 = lambda: pl.BlockSpec((1, 8, HEAD_DIM), lambda b, pt: (b, 0, 0))
    vec = pl.BlockSpec((1, HEAD_DIM), lambda b, pt: (0, 0))
    hbm = lambda: pl.BlockSpec(memory_space=pl.ANY)
    grid_spec = pltpu.PrefetchScalarGridSpec(
        num_scalar_prefetch=1,
        grid=(dec_batch,),
        in_specs=[row8(), row8(), row8(), row8(), row8(),
                  pl.BlockSpec((8, HEAD_DIM), lambda b, pt: (0, 0)),
                  vec, vec, vec, vec,
                  hbm(), hbm(), hbm()],
        out_specs=(pl.BlockSpec((1, 8, HEAD_DIM), lambda b, pt: (b, 0, 0)),
                   pl.BlockSpec((1, 8, HEAD_DIM), lambda b, pt: (b, 0, 0))),
        scratch_shapes=[pltpu.VMEM((2, rows, HEAD_DIM), F32), pltpu.VMEM((2, rows, HEAD_DIM), F32),
                        pltpu.VMEM((2, rows, HEAD_DIM), F32), pltpu.SemaphoreType.DMA((3, 2)),
                        pltpu.VMEM((n_blocks, 8, HEAD_DIM), F32)],
    )
    return pl.pallas_call(
        functools.partial(_dec1_kernel, n_pages=n_pages, past_len=past_len),
        grid_spec=grid_spec,
        out_shape=(jax.ShapeDtypeStruct((dec_batch, 8, HEAD_DIM), BF16),
                   jax.ShapeDtypeStruct((dec_batch, 8, HEAD_DIM), jnp.int32)),
        compiler_params=pltpu.CompilerParams(
            dimension_semantics=("arbitrary",), vmem_limit_bytes=VMEM_LIMIT_V7X),
        name="dec1",
    )(page_table, qd8, qm8, kdn8, vdn8, zd8, gn8, *lams, ck3, cv3, cmk3)


def _dec2_kernel(pt_ref, sel_ref, qm_ref, kmn_ref, vmn_ref, zm_ref, ck_hbm, cv_hbm, am_ref, kbuf, vbuf, sems, o_sc,
                 *, past_len):
    g = pl.program_id(0)
    n_g = pl.num_programs(0)
    h = g % MOBA_HEADS
    slot = g % 2
    n_pg = MOBA_TOPK * PAGES_PER_BLOCK
    rows = n_pg * ROWS_PG

    def head_copies(gg, sl):
        bb = gg // MOBA_HEADS
        hh = gg % MOBA_HEADS
        out = []
        for t in range(MOBA_TOPK):
            blk = sel_ref[bb, hh * MOBA_TOPK + t]
            for f in range(PAGES_PER_BLOCK):
                page = pt_ref[bb, blk * PAGES_PER_BLOCK + f]
                dst = pl.ds((t * PAGES_PER_BLOCK + f) * ROWS_PG, ROWS_PG)
                out.append(pltpu.make_async_copy(ck_hbm.at[page], kbuf.at[sl, dst, :], sems.at[0, sl]))
                out.append(pltpu.make_async_copy(cv_hbm.at[page], vbuf.at[sl, dst, :], sems.at[1, sl]))
        return out

    @pl.when(g == 0)
    def _():
        for cp in head_copies(0, 0):
            cp.start()

    @pl.when(g + 1 < n_g)
    def _():
        for cp in head_copies(g + 1, 1 - slot):
            cp.start()

    for cp in head_copies(g, slot):
        cp.wait()

    q8 = qm_ref[0]
    row = lax.broadcasted_iota(jnp.int32, (8, rows), 0)
    col = lax.broadcasted_iota(jnp.int32, (8, rows), 1)
    row1 = lax.broadcasted_iota(jnp.int32, (8, HEAD_DIM), 0)
    b = g // MOBA_HEADS
    blk_col = jnp.zeros((8, rows), jnp.int32)
    for t in range(MOBA_TOPK):
        seg = (col >= t * PAGES_PER_BLOCK * ROWS_PG) & (col < (t + 1) * PAGES_PER_BLOCK * ROWS_PG)
        blk_col = jnp.where(seg, sel_ref[b, h * MOBA_TOPK + t], blk_col)
    kpos = blk_col * MOBA_BLOCK + ((col & (PAGES_PER_BLOCK * ROWS_PG - 1)) >> 2)
    s = lax.dot_general(q8, kbuf[slot].astype(BF16), _NT, preferred_element_type=F32) * SCALE
    s = s - _pow2_neg(row + 1) * (past_len - kpos).astype(F32)
    valid = (row == h) & ((col & 3) == (h >> 1))
    s = jnp.where(valid, s, NEG_INF)
    s_new = jnp.sum(q8.astype(F32) * kmn_ref[0], axis=1, keepdims=True) * SCALE
    m = jnp.maximum(jnp.max(s, axis=1, keepdims=True), s_new)
    p = jnp.where(valid, jnp.exp(s - m), 0.0)
    p_new = jnp.exp(s_new - m)
    l = jnp.sum(p, axis=1, keepdims=True) + p_new
    o = (jnp.dot(p.astype(BF16), vbuf[slot].astype(BF16), preferred_element_type=F32) + p_new * vmn_ref[0]) / l

    @pl.when(h == 0)
    def _():
        o_sc[...] = jnp.zeros(o_sc.shape, F32)

    o_sc[...] = jnp.where(row1 == h, o, o_sc[...])

    @pl.when(h == MOBA_HEADS - 1)
    def _():
        am_ref[0] = (o_sc[...] * zm_ref[0].astype(F32)).astype(BF16)


def _dec2(page_table, sel, qm8, kmn8, vmn8, zm8, cmk3, cmv3, past_len):
    dec_batch = page_table.shape[0]
    rows = MOBA_TOPK * PAGES_PER_BLOCK * ROWS_PG
    row8 = lambda: pl.BlockSpec((1, 8, HEAD_DIM), lambda g, pt, sl: (g // MOBA_HEADS, 0, 0))
    hbm = lambda: pl.BlockSpec(memory_space=pl.ANY)
    grid_spec = pltpu.PrefetchScalarGridSpec(
        num_scalar_prefetch=2,
        grid=(dec_batch * MOBA_HEADS,),
        in_specs=[row8(), row8(), row8(), row8(), hbm(), hbm()],
        out_specs=pl.BlockSpec((1, 8, HEAD_DIM), lambda g, pt, sl: (g // MOBA_HEADS, 0, 0)),
        scratch_shapes=[pltpu.VMEM((2, rows, HEAD_DIM), F32), pltpu.VMEM((2, rows, HEAD_DIM), F32),
                        pltpu.SemaphoreType.DMA((2, 2)), pltpu.VMEM((8, HEAD_DIM), F32)],
    )
    return pl.pallas_call(
        functools.partial(_dec2_kernel, past_len=past_len),
        grid_spec=grid_spec,
        out_shape=jax.ShapeDtypeStruct((dec_batch, 8, HEAD_DIM), BF16),
        compiler_params=pltpu.CompilerParams(
            dimension_semantics=("arbitrary",), vmem_limit_bytes=VMEM_LIMIT_V7X),
        name="dec2",
    )(page_table, sel, qm8, kmn8, vmn8, zm8, cmk3, cmv3)


def _post_kernel(am_ref, ad_ref, g_ref, x_ref, wmo_ref, wdo_ref, wo_ref, lng_ref, lnb_ref, y_ref):
    ym = jnp.dot(am_ref[...], wmo_ref[...], preferred_element_type=F32)
    yd = jnp.dot(ad_ref[...], wdo_ref[...], preferred_element_type=F32)
    g = g_ref[...].astype(F32)
    merged = g[:, :D_MODEL] * ym + g[:, D_MODEL:] * yd
    f = jnp.dot(merged.astype(BF16), wo_ref[...], preferred_element_type=F32)
    u = ALPHA * x_ref[...] + f
    mu = jnp.mean(u, axis=1, keepdims=True)
    d = u - mu
    var = jnp.mean(d * d, axis=1, keepdims=True)
    y_ref[...] = d * lax.rsqrt(var + LN_EPS) * lng_ref[...] + lnb_ref[...]


def _post(am, ad, gt, x2d, wmo, wdo, wo, lng, lnb, tm):
    m_rows = x2d.shape[0]
    rowblk = lambda w: pl.BlockSpec((tm, w), lambda m: (m, 0))
    full = lambda a: pl.BlockSpec(a.shape, lambda m: (0, 0))
    return pl.pallas_call(
        _post_kernel,
        grid=(m_rows // tm,),
        in_specs=[rowblk(1024), rowblk(1024), rowblk(N_BRANCH * D_MODEL), rowblk(D_MODEL),
                  full(wmo), full(wdo), full(wo), full(lng), full(lnb)],
        out_specs=rowblk(D_MODEL),
        out_shape=jax.ShapeDtypeStruct((m_rows, D_MODEL), F32),
        compiler_params=pltpu.CompilerParams(
            dimension_semantics=("parallel",), vmem_limit_bytes=VMEM_LIMIT_V7X),
        name="post",
    )(am, ad, gt, x2d, wmo, wdo, wo, lng, lnb)


def kernel(x_prompt, x_sample, cache_moba_k, cache_moba_v, cache_diff_k, cache_diff_v, page_table, w_in, w_moba_o, w_diff_o, w_out, lambda_q1, lambda_k1, lambda_q2, lambda_k2, diff_norm_g, ln_g, ln_b):
    assert w_in.shape[0] == DEPTH == 1
    batch, seq, _ = x_prompt.shape
    dec_batch, dec_seq, _ = x_sample.shape
    assert dec_seq == 1 and seq % MOBA_BLOCK == 0
    n_pages = page_table.shape[1]
    past_len = n_pages * PAGE_SIZE
    assert past_len % MOBA_BLOCK == 0 and past_len // MOBA_BLOCK >= MOBA_TOPK
    n_phys = cache_moba_k.shape[1]

    w_bf = w_in[0].astype(BF16)
    wqt_bf = jnp.concatenate([w_in[0][:, 0:1024], w_in[0][:, 3072:4096]], axis=1).T.astype(BF16)
    wmo = w_moba_o[0].astype(BF16)
    wdo = w_diff_o[0].astype(BF16)
    wo = w_out[0].astype(BF16)
    lams = (lambda_q1, lambda_k1, lambda_q2, lambda_k2)
    lng = ln_g.reshape(1, D_MODEL)
    lnb = ln_b.reshape(1, D_MODEL)

    xp = x_prompt.reshape(batch * seq, D_MODEL)
    qmt, km, vm, zm, qdt, kd, vd, zd, gt = _proj(xp, w_bf, wqt_bf, tm=512)
    am = _moba_prompt(qmt, km, vm, zm, batch, seq)
    ad = _diff_prompt(qdt, kd, vd, zd, diff_norm_g.reshape(DIFF_KV_HEADS, 2, DIFF_V_DIM), lams, batch, seq)
    y_p = _post(am, ad, gt, xp, wmo, wdo, wo, lng, lnb, tm=256)

    xs = x_sample.reshape(dec_batch, D_MODEL)
    qmt_s, km_s, vm_s, zm_s, qdt_s, kd_s, vd_s, zd_s, gt_s = _proj(xs, w_bf, wqt_bf, tm=dec_batch)
    cmk3 = cache_moba_k.reshape(n_phys, ROWS_PG, HEAD_DIM)
    cmv3 = cache_moba_v.reshape(n_phys, ROWS_PG, HEAD_DIM)

    def diff_pages(c):
        c = c.reshape(n_phys, PAGE_SIZE, DIFF_KV_HEADS, 2, HEAD_DIM)
        return jnp.transpose(c, (0, 1, 3, 2, 4)).reshape(n_phys, ROWS_PG, HEAD_DIM)

    cdk3 = diff_pages(cache_diff_k)
    cdv3 = diff_pages(cache_diff_v)

    def pair_rows(a):
        a = a.reshape(dec_batch, DIFF_KV_HEADS, 1, 2, HEAD_DIM)
        return jnp.broadcast_to(a, (dec_batch, DIFF_KV_HEADS, 2, 2, HEAD_DIM)).reshape(dec_batch, 8, HEAD_DIM)

    qd8 = qdt_s.T.reshape(dec_batch, 8, HEAD_DIM)
    qm8 = qmt_s.T.reshape(dec_batch, 8, HEAD_DIM)
    zd8 = zd_s.reshape(dec_batch, 8, HEAD_DIM)
    zm8 = zm_s.reshape(dec_batch, 8, HEAD_DIM)
    gn8 = diff_norm_g.reshape(8, HEAD_DIM)
    ad_s, sel = _dec1(page_table, qd8, qm8, pair_rows(kd_s), pair_rows(vd_s), zd8, gn8, lams,
                      cdk3, cdv3, cmk3, past_len)
    sel2 = sel[:, :, :MOBA_TOPK].reshape(dec_batch, MOBA_HEADS * MOBA_TOPK)
    kmn8 = jnp.repeat(km_s.reshape(dec_batch, MOBA_KV_HEADS, HEAD_DIM), 2, axis=1)
    vmn8 = jnp.repeat(vm_s.reshape(dec_batch, MOBA_KV_HEADS, HEAD_DIM), 2, axis=1)
    am_s = _dec2(page_table, sel2, qm8, kmn8, vmn8, zm8, cmk3, cmv3, past_len)
    y_s = _post(am_s.reshape(dec_batch, 1024), ad_s.reshape(dec_batch, 1024), gt_s, xs,
                wmo, wdo, wo, lng, lnb, tm=dec_batch)

    return (y_p.reshape(batch, seq, D_MODEL), y_s.reshape(dec_batch, 1, D_MODEL),
            km.reshape(DEPTH, batch, seq, MOBA_KV_HEADS, HEAD_DIM),
            vm.reshape(DEPTH, batch, seq, MOBA_KV_HEADS, HEAD_DIM),
            kd.reshape(DEPTH, batch, seq, DIFF_KV_HEADS, 2 * HEAD_DIM),
            vd.reshape(DEPTH, batch, seq, DIFF_KV_HEADS, DIFF_V_DIM),
            km_s.reshape(DEPTH, dec_batch, 1, MOBA_KV_HEADS, HEAD_DIM),
            vm_s.reshape(DEPTH, dec_batch, 1, MOBA_KV_HEADS, HEAD_DIM),
            kd_s.reshape(DEPTH, dec_batch, 1, DIFF_KV_HEADS, 2 * HEAD_DIM),
            vd_s.reshape(DEPTH, dec_batch, 1, DIFF_KV_HEADS, DIFF_V_DIM))
```

```python
import functools
import math

import jax
import jax.numpy as jnp
from jax import lax
from jax.experimental import pallas as pl
from jax.experimental.pallas import tpu as pltpu

F32 = jnp.float32
BF16 = jnp.bfloat16

D_MODEL = 2048
HEAD_DIM = 128
PAGE_SIZE = 128
MOBA_HEADS = 8
MOBA_KV_HEADS = 4
MOBA_BLOCK = 256
MOBA_TOPK = 3
DIFF_HEADS = 4
DIFF_KV_HEADS = 2
DIFF_V_DIM = 2 * HEAD_DIM
N_BRANCH = 2
DEPTH = 1
ALPHA = (2 * DEPTH) ** 0.25
LN_EPS = 1e-5
RMS_EPS = 1e-5
NEG_INF = -1e30
SCALE = HEAD_DIM ** -0.5
LAM0 = 0.8 - 0.6 * math.exp(-0.3 * 0)

VMEM_LIMIT_V7X = 56 * 1024 * 1024

PROJ_TN = 1024

_NT = (((1,), (1,)), ((), ()))


def _pow2_neg(k):
    return lax.bitcast_convert_type((127 - k) << 23, F32)


def _silu(v):
    return v * (1.0 / (1.0 + jnp.exp(-v)))


def _sigmoid(v):
    return 1.0 / (1.0 + jnp.exp(-v))


def _proj_kernel(x_ref, w_ref, qm_ref, km_ref, vm_ref, zm_ref, qd_ref, kd_ref, vd_ref, zd_ref, gt_ref, wb_sc):
    n = pl.program_id(0)
    tm = x_ref.shape[0]

    @pl.when(pl.program_id(1) == 0)
    def _():
        wb_sc[...] = w_ref[...].astype(BF16)

    def tile():
        return jnp.dot(x_ref[...].astype(BF16), wb_sc[...], preferred_element_type=F32)

    half = PROJ_TN // 2

    @pl.when(n == 0)
    def _():
        qm_ref[...] = tile().astype(BF16)

    @pl.when(n == 1)
    def _():
        acc = tile()
        for h in range(MOBA_KV_HEADS):
            rows_h = pl.ds(h, tm, stride=MOBA_KV_HEADS)
            km_ref[rows_h, :] = acc[:, h * HEAD_DIM:(h + 1) * HEAD_DIM]
            vm_ref[rows_h, :] = acc[:, half + h * HEAD_DIM:half + (h + 1) * HEAD_DIM]

    @pl.when(n == 2)
    def _():
        zm_ref[...] = _silu(tile()).astype(BF16)

    @pl.when(n == 3)
    def _():
        qd_ref[...] = tile().astype(BF16)

    @pl.when(n == 4)
    def _():
        acc = tile()
        for hk in range(DIFF_KV_HEADS):
            for c in range(2):
                rows_c = pl.ds(c * DIFF_KV_HEADS + hk, tm, stride=2 * DIFF_KV_HEADS)
                lo = hk * DIFF_V_DIM + c * HEAD_DIM
                kd_ref[rows_c, :] = acc[:, lo:lo + HEAD_DIM]
                vd_ref[rows_c, :] = acc[:, half + lo:half + lo + HEAD_DIM]

    @pl.when(n == 5)
    def _():
        zd_ref[...] = _silu(tile()).astype(BF16)

    @pl.when(n >= 6)
    def _():
        gt_ref[...] = _sigmoid(tile()).astype(BF16)


def _proj(x2d, w, tm):
    m_rows = x2d.shape[0]
    n_tiles = w.shape[1] // PROJ_TN
    m_last = m_rows // tm - 1

    def out_map(lo, hi):
        return lambda n, m: (jnp.where(n < lo, 0, jnp.where(n >= hi, m_last, m)), jnp.clip(n - lo, 0, hi - lo - 1))

    out_shape = (
        jax.ShapeDtypeStruct((m_rows, 1024), BF16),
        jax.ShapeDtypeStruct((m_rows * 4, HEAD_DIM), F32),
        jax.ShapeDtypeStruct((m_rows * 4, HEAD_DIM), F32),
        jax.ShapeDtypeStruct((m_rows, 1024), BF16),
        jax.ShapeDtypeStruct((m_rows, 1024), BF16),
        jax.ShapeDtypeStruct((m_rows * 4, HEAD_DIM), F32),
        jax.ShapeDtypeStruct((m_rows * 4, HEAD_DIM), F32),
        jax.ShapeDtypeStruct((m_rows, 1024), BF16),
        jax.ShapeDtypeStruct((m_rows, 4096), BF16),
    )
    out_specs = (
        pl.BlockSpec((tm, PROJ_TN), out_map(0, 1)),
        pl.BlockSpec((tm * 4, HEAD_DIM), out_map(1, 2)),
        pl.BlockSpec((tm * 4, HEAD_DIM), out_map(1, 2)),
        pl.BlockSpec((tm, PROJ_TN), out_map(2, 3)),
        pl.BlockSpec((tm, PROJ_TN), out_map(3, 4)),
        pl.BlockSpec((tm * 4, HEAD_DIM), out_map(4, 5)),
        pl.BlockSpec((tm * 4, HEAD_DIM), out_map(4, 5)),
        pl.BlockSpec((tm, PROJ_TN), out_map(5, 6)),
        pl.BlockSpec((tm, PROJ_TN), out_map(6, 10)),
    )
    return pl.pallas_call(
        _proj_kernel,
        grid=(n_tiles, m_rows // tm),
        in_specs=[pl.BlockSpec((tm, D_MODEL), lambda n, m: (m, 0)),
                  pl.BlockSpec((D_MODEL, PROJ_TN), lambda n, m: (0, n))],
        out_specs=out_specs,
        out_shape=out_shape,
        scratch_shapes=[pltpu.VMEM((D_MODEL, PROJ_TN), BF16)],
        compiler_params=pltpu.CompilerParams(
            dimension_semantics=("arbitrary", "arbitrary"), vmem_limit_bytes=VMEM_LIMIT_V7X),
        name="proj",
    )(x2d, w)


LOG2E = 1.4426950408889634
SCALE_L2 = SCALE * LOG2E
MASKED = 1e30


def _alibi_tiles_t(blk):
    c_i = lax.broadcasted_iota(jnp.int32, (blk, 2 * blk), 0)
    q_i = lax.broadcasted_iota(jnp.int32, (blk, 2 * blk), 1)
    g = q_i // blk
    rel = (q_i - g * blk - c_i).astype(F32)
    return rel, g[:1], rel >= 0.0


def _slope_row_l2(g_row, head0, slope_mul):
    return _pow2_neg(slope_mul * (head0 + g_row + 1)) * LOG2E


def _moba_prompt_kernel(q_ref, k_ref, v_ref, z_ref, o_ref, kb_sc, vt_sc, km_sc, bias_sc, acc_sc, *, n_blk):
    j = pl.program_id(1)
    blk = MOBA_BLOCK
    n_kv = MOBA_KV_HEADS
    rel, g_row, causal = _alibi_tiles_t(blk)

    @pl.when(j == 0)
    def _():
        for hk in range(n_kv):
            k = k_ref[pl.ds(hk, n_blk * blk, stride=n_kv), :].reshape(n_blk, blk, HEAD_DIM)
            kb_sc[hk] = k.astype(BF16)
            km_sc[hk] = jnp.sum(k, axis=1) * (1.0 / blk)
            for n in range(n_blk):
                vt_sc[hk, n] = v_ref[pl.ds(n * blk * n_kv + hk, blk, stride=n_kv), :].T.astype(BF16)
            bias_sc[hk] = rel * _slope_row_l2(g_row, 2 * hk, 1)

    qss, ranks, slope_blks, init = [], [], [], []
    for hk in range(n_kv):
        qj = q_ref[0, :, hk * 2 * HEAD_DIM:(hk + 1) * 2 * HEAD_DIM]
        qs = jnp.concatenate([qj[:, :HEAD_DIM], qj[:, HEAD_DIM:]], axis=0)
        kmean = km_sc[hk]
        kmean_hi = kmean.astype(BF16)
        kmean_lo = (kmean - kmean_hi.astype(F32)).astype(BF16)
        gate = (lax.dot_general(kmean_hi, qs, _NT, preferred_element_type=F32)
                + lax.dot_general(kmean_lo, qs, _NT, preferred_element_type=F32))
        sub = lax.broadcasted_iota(jnp.int32, gate.shape, 0)
        gate = jnp.where(sub < j, gate, NEG_INF)
        rank = jnp.zeros(gate.shape, F32)
        for m in range(n_blk):
            gm = gate[m:m + 1, :]
            ge = jnp.where(gm >= gate, 1.0, 0.0)
            gt = jnp.where(gm > gate, 1.0, 0.0)
            rank = rank + jnp.where(sub > m, ge, gt)
        u = lax.dot_general(kb_sc[hk, j], qs, _NT, preferred_element_type=F32) * SCALE_L2 - bias_sc[hk]
        u = jnp.where(causal, u, -MASKED)
        m0 = jnp.max(u, axis=0, keepdims=True)
        p = jnp.exp2(u - m0)
        acc_sc[hk] = jnp.dot(vt_sc[hk, j], p.astype(BF16), preferred_element_type=F32)
        qss.append(qs)
        ranks.append(rank)
        slope_blks.append(_slope_row_l2(g_row, 2 * hk, 1) * float(blk))
        init += [m0, jnp.sum(p, axis=0, keepdims=True)]

    def body(n, carry):
        dist_blocks = (j - n).astype(F32)
        out = []
        for hk in range(n_kv):
            m_run, l_run = carry[2 * hk:2 * hk + 2]
            u = lax.dot_general(kb_sc[hk, n], qss[hk], _NT, preferred_element_type=F32) * SCALE_L2 - bias_sc[hk]
            sub = lax.broadcasted_iota(jnp.int32, ranks[hk].shape, 0)
            rk = jnp.sum(jnp.where(sub == n, ranks[hk], 0.0), axis=0, keepdims=True)
            rowc = jnp.where(rk < float(MOBA_TOPK), slope_blks[hk] * dist_blocks, MASKED)
            m_new = jnp.maximum(m_run, jnp.max(u, axis=0, keepdims=True) - rowc)
            alpha = jnp.exp2(m_run - m_new)
            p = jnp.exp2(u - (rowc + m_new))
            out += [m_new, alpha * l_run + jnp.sum(p, axis=0, keepdims=True)]
            acc_sc[hk] = alpha * acc_sc[hk] + jnp.dot(vt_sc[hk, n], p.astype(BF16), preferred_element_type=F32)
        return tuple(out)

    fin = lax.fori_loop(0, j, body, tuple(init))
    for hk in range(n_kv):
        ot = acc_sc[hk] / fin[2 * hk + 1]
        o2 = jnp.concatenate([ot[:, :blk].T, ot[:, blk:].T], axis=1)
        cols = slice(hk * 2 * HEAD_DIM, (hk + 1) * 2 * HEAD_DIM)
        o_ref[0, :, cols] = (o2 * z_ref[0, :, cols].astype(F32)).astype(BF16)


def _moba_prompt(qm, km, vm, zm, batch, seq):
    n_blk = seq // MOBA_BLOCK
    blk = MOBA_BLOCK
    width = MOBA_HEADS * HEAD_DIM
    k3 = km.reshape(batch, seq * MOBA_KV_HEADS, HEAD_DIM)
    v3 = vm.reshape(batch, seq * MOBA_KV_HEADS, HEAD_DIM)
    z3 = zm.reshape(batch, seq, width)
    q3 = qm.reshape(batch, seq, width)
    out = pl.pallas_call(
        functools.partial(_moba_prompt_kernel, n_blk=n_blk),
        grid=(batch, n_blk),
        in_specs=[pl.BlockSpec((1, blk, width), lambda b, j: (b, j, 0)),
                  pl.BlockSpec((None, seq * MOBA_KV_HEADS, HEAD_DIM), lambda b, j: (b, 0, 0)),
                  pl.BlockSpec((None, seq * MOBA_KV_HEADS, HEAD_DIM), lambda b, j: (b, 0, 0)),
                  pl.BlockSpec((1, blk, width), lambda b, j: (b, j, 0))],
        out_specs=pl.BlockSpec((1, blk, width), lambda b, j: (b, j, 0)),
        out_shape=jax.ShapeDtypeStruct((batch, seq, width), BF16),
        scratch_shapes=[pltpu.VMEM((MOBA_KV_HEADS, n_blk, blk, HEAD_DIM), BF16),
                        pltpu.VMEM((MOBA_KV_HEADS, n_blk, HEAD_DIM, blk), BF16),
                        pltpu.VMEM((MOBA_KV_HEADS, n_blk, HEAD_DIM), F32),
                        pltpu.VMEM((MOBA_KV_HEADS, blk, 2 * blk), F32),
                        pltpu.VMEM((MOBA_KV_HEADS, HEAD_DIM, 2 * blk), F32)],
        compiler_params=pltpu.CompilerParams(
            dimension_semantics=("parallel", "arbitrary"), vmem_limit_bytes=VMEM_LIMIT_V7X),
        name="moba_prompt",
    )(q3, k3, v3, z3)
    return out.reshape(batch * seq, width)


def _lambda_full(lq1_ref, lk1_ref, lq2_ref, lk2_ref):
    a = jnp.sum(lq1_ref[...] * lk1_ref[...], axis=1, keepdims=True)
    b = jnp.sum(lq2_ref[...] * lk2_ref[...], axis=1, keepdims=True)
    return jnp.exp(a) - jnp.exp(b) + LAM0


def _diff_prompt_kernel(q_ref, k_ref, v_ref, z_ref, gn_ref, lq1_ref, lk1_ref, lq2_ref, lk2_ref, o_ref,
                        kb_sc, vt_sc, bias_sc, acc_sc, *, n_blk):
    j = pl.program_id(1)
    blk = MOBA_BLOCK
    n_kv = DIFF_KV_HEADS
    rel, g_row, causal = _alibi_tiles_t(blk)

    @pl.when(j == 0)
    def _():
        for hk in range(n_kv):
            for c in range(2):
                kc = k_ref[pl.ds(c * n_kv + hk, n_blk * blk, stride=2 * n_kv), :]
                kb_sc[hk, :, :, c * HEAD_DIM:(c + 1) * HEAD_DIM] = kc.reshape(n_blk, blk, HEAD_DIM).astype(BF16)
                for n in range(n_blk):
                    vc = v_ref[pl.ds(n * blk * 2 * n_kv + c * n_kv + hk, blk, stride=2 * n_kv), :]
                    vt_sc[hk, n, c * HEAD_DIM:(c + 1) * HEAD_DIM, :] = vc.T.astype(BF16)
            bias_sc[hk] = rel * _slope_row_l2(g_row, 2 * hk, 2)

    lam = _lambda_full(lq1_ref, lk1_ref, lq2_ref, lk2_ref)
    chains = [(hk, mi) for hk in range(n_kv) for mi in range(2)]
    qss = {}
    for hk in range(n_kv):
        qj = q_ref[0, :, hk * 4 * HEAD_DIM:(hk + 1) * 4 * HEAD_DIM]
        for mi in range(2):
            qss[hk, mi] = jnp.concatenate([qj[:, mi * HEAD_DIM:(mi + 1) * HEAD_DIM],
                                           qj[:, (2 + mi) * HEAD_DIM:(3 + mi) * HEAD_DIM]], axis=0)
    slope_blk = [_slope_row_l2(g_row, 2 * hk, 2) * float(blk) for hk in range(n_kv)]

    def scores(hk, mi, n):
        kn = kb_sc[hk, n, :, mi * HEAD_DIM:(mi + 1) * HEAD_DIM]
        return lax.dot_general(kn, qss[hk, mi], _NT, preferred_element_type=F32) * SCALE_L2 - bias_sc[hk]

    init = []
    for ci, (hk, mi) in enumerate(chains):
        u = jnp.where(causal, scores(hk, mi, j), -MASKED)
        m0 = jnp.max(u, axis=0, keepdims=True)
        p = jnp.exp2(u - m0)
        init += [m0, jnp.sum(p, axis=0, keepdims=True)]
        acc_sc[ci] = jnp.dot(vt_sc[hk, j], p.astype(BF16), preferred_element_type=F32)

    def body(n, carry):
        dist_blocks = (j - n).astype(F32)
        out = []
        for ci, (hk, mi) in enumerate(chains):
            m_run, l_run = carry[2 * ci:2 * ci + 2]
            rowc = slope_blk[hk] * dist_blocks
            u = scores(hk, mi, n)
            m_new = jnp.maximum(m_run, jnp.max(u, axis=0, keepdims=True) - rowc)
            alpha = jnp.exp2(m_run - m_new)
            p = jnp.exp2(u - (rowc + m_new))
            out += [m_new, alpha * l_run + jnp.sum(p, axis=0, keepdims=True)]
            acc_sc[ci] = alpha * acc_sc[ci] + jnp.dot(vt_sc[hk, n], p.astype(BF16), preferred_element_type=F32)
        return tuple(out)

    fin = lax.fori_loop(0, j, body, tuple(init))
    for hk in range(n_kv):
        c1, c2 = 2 * hk, 2 * hk + 1
        ot = acc_sc[c1] / fin[2 * c1 + 1] - lam * (acc_sc[c2] / fin[2 * c2 + 1])
        ms = jnp.mean(ot * ot, axis=0, keepdims=True)
        ot = ot * lax.rsqrt(ms + RMS_EPS)
        gn = gn_ref[hk]
        o2 = jnp.concatenate([ot[:, :blk].T * gn[0:1, :], ot[:, blk:].T * gn[1:2, :]], axis=1) * (1.0 - LAM0)
        cols = slice(hk * 2 * DIFF_V_DIM, (hk + 1) * 2 * DIFF_V_DIM)
        o_ref[0, :, cols] = (o2 * z_ref[0, :, cols].astype(F32)).astype(BF16)


def _diff_prompt(qd, kd, vd, zd, gn3, lams, batch, seq):
    n_blk = seq // MOBA_BLOCK
    blk = MOBA_BLOCK
    width = DIFF_HEADS * DIFF_V_DIM
    k3 = kd.reshape(batch, seq * 4, HEAD_DIM)
    v3 = vd.reshape(batch, seq * 4, HEAD_DIM)
    z3 = zd.reshape(batch, seq, width)
    q3 = qd.reshape(batch, seq, width)
    vec = pl.BlockSpec((1, HEAD_DIM), lambda b, j: (0, 0))
    out = pl.pallas_call(
        functools.partial(_diff_prompt_kernel, n_blk=n_blk),
        grid=(batch, n_blk),
        in_specs=[pl.BlockSpec((1, blk, width), lambda b, j: (b, j, 0)),
                  pl.BlockSpec((None, seq * 4, HEAD_DIM), lambda b, j: (b, 0, 0)),
                  pl.BlockSpec((None, seq * 4, HEAD_DIM), lambda b, j: (b, 0, 0)),
                  pl.BlockSpec((1, blk, width), lambda b, j: (b, j, 0)),
                  pl.BlockSpec((DIFF_KV_HEADS, 2, DIFF_V_DIM), lambda b, j: (0, 0, 0)),
                  vec, vec, vec, vec],
        out_specs=pl.BlockSpec((1, blk, width), lambda b, j: (b, j, 0)),
        out_shape=jax.ShapeDtypeStruct((batch, seq, width), BF16),
        scratch_shapes=[pltpu.VMEM((DIFF_KV_HEADS, n_blk, blk, 2 * HEAD_DIM), BF16),
                        pltpu.VMEM((DIFF_KV_HEADS, n_blk, DIFF_V_DIM, blk), BF16),
                        pltpu.VMEM((DIFF_KV_HEADS, blk, 2 * blk), F32),
                        pltpu.VMEM((2 * DIFF_KV_HEADS, DIFF_V_DIM, 2 * blk), F32)],
        compiler_params=pltpu.CompilerParams(
            dimension_semantics=("parallel", "arbitrary"), vmem_limit_bytes=VMEM_LIMIT_V7X),
        name="diff_prompt",
    )(q3, k3, v3, z3, gn3, *lams)
    return out.reshape(batch * seq, width)


ROWS_PG = PAGE_SIZE * 4
DEC1_PAGES_PER_CHUNK = 16
PAGES_PER_BLOCK = MOBA_BLOCK // PAGE_SIZE


def _pair_lo(x, even):
    return jnp.where(even, x, pltpu.roll(x, 1, axis=0))


def _pair_hi(x, even):
    return jnp.where(even, pltpu.roll(x, x.shape[0] - 1, axis=0), x)


def _dec1_kernel(pt_ref, qd_ref, qm_ref, kdn_ref, vdn_ref, zd_ref, gn_ref, lq1_ref, lk1_ref, lq2_ref, lk2_ref,
                 ck_hbm, cv_hbm, cmk_hbm, ad_ref, sel_ref, kbuf, vbuf, mbuf, sems, ks_sc,
                 *, n_pages, past_len):
    b = pl.program_id(0)
    n_b = pl.num_programs(0)
    ppc = DEC1_PAGES_PER_CHUNK
    n_chunks = n_pages // ppc
    rows = ppc * ROWS_PG
    streams = ((ck_hbm, kbuf), (cv_hbm, vbuf), (cmk_hbm, mbuf))

    def chunk_copies(bb, c, slot):
        out = []
        for k in range(ppc):
            page = pt_ref[bb, c * ppc + k]
            for si, (src, dst) in enumerate(streams):
                out.append(pltpu.make_async_copy(
                    src.at[page], dst.at[slot, pl.ds(k * ROWS_PG, ROWS_PG), :], sems.at[si, slot]))
        return out

    @pl.when(b == 0)
    def _():
        for cp in chunk_copies(0, 0, 0):
            cp.start()

    q8 = qd_ref[0]
    row = lax.broadcasted_iota(jnp.int32, (8, rows), 0)
    col = lax.broadcasted_iota(jnp.int32, (8, rows), 1)
    row1 = lax.broadcasted_iota(jnp.int32, (8, HEAD_DIM), 0)
    even = (row & 1) == 0
    even1 = (row1 & 1) == 0
    valid = (col & 3) == ((row & 1) * 2 + (row >> 2))
    slope = _pow2_neg(2 * ((row >> 1) + 1))
    posf = (col >> 2).astype(F32)

    def chunk_compute(c, slot, carry):
        m_old, l_old, acc = carry
        s = lax.dot_general(q8, kbuf[slot].astype(BF16), _NT, preferred_element_type=F32) * SCALE
        base = jnp.asarray(past_len - c * (ppc * PAGE_SIZE)).astype(F32)
        s = jnp.where(valid, s - slope * (base - posf), NEG_INF)
        m_new = jnp.maximum(m_old, jnp.max(s, axis=1, keepdims=True))
        alpha = jnp.exp(m_old - m_new)
        p = jnp.where(valid, jnp.exp(s - m_new[:, :1]), 0.0)
        l_new = alpha * l_old + jnp.sum(p, axis=1, keepdims=True)
        p_r = pltpu.roll(p, 2, axis=1)
        p_l = pltpu.roll(p, rows - 2, axis=1)
        w1 = jnp.where(even, p, pltpu.roll(p_r, 1, axis=0))
        w2 = jnp.where(even, pltpu.roll(p_l, 7, axis=0), p)
        w = jnp.concatenate([w1, w2], axis=0).astype(BF16)
        a16 = jnp.concatenate([_pair_lo(alpha, even1), _pair_hi(alpha, even1)], axis=0)
        acc = a16 * acc + jnp.dot(w, vbuf[slot].astype(BF16), preferred_element_type=F32)
        bpc = ppc // PAGES_PER_BLOCK
        ksum = jnp.sum(mbuf[slot].reshape(bpc, PAGES_PER_BLOCK * ROWS_PG // 8, 8, HEAD_DIM), axis=1)
        ks_sc[pl.ds(c * bpc, bpc)] = ksum
        return m_new, l_new, acc

    def pair_body(c2, carry):
        for u in range(2):
            c = 2 * c2 + u
            in_b = c + 1 < n_chunks

            @pl.when(in_b | (b + 1 < n_b))
            def _():
                for cp in chunk_copies(jnp.where(in_b, b, b + 1), jnp.where(in_b, c + 1, 0), 1 - u):
                    cp.start()

            for cp in chunk_copies(b, c, u):
                cp.wait()
            carry = chunk_compute(c, u, carry)
        return carry

    init = (jnp.full((8, HEAD_DIM), NEG_INF, F32), jnp.zeros((8, HEAD_DIM), F32), jnp.zeros((16, HEAD_DIM), F32))
    m_o, l_o, acc = lax.fori_loop(0, n_chunks // 2, pair_body, init)

    lam = _lambda_full(lq1_ref, lk1_ref, lq2_ref, lk2_ref)
    qf = q8.astype(F32)
    s_new = jnp.sum(qf * kdn_ref[0], axis=1, keepdims=True) * SCALE
    m_f = jnp.maximum(m_o, s_new)
    a_f = jnp.exp(m_o - m_f)
    p_new = jnp.exp(s_new - m_f)
    l_f = a_f * l_o + p_new
    vdn = vdn_ref[0]
    acc1 = _pair_lo(a_f, even1) * acc[:8] + _pair_lo(p_new, even1) * vdn
    acc2 = _pair_hi(a_f, even1) * acc[8:] + _pair_hi(p_new, even1) * vdn
    o = acc1 / _pair_lo(l_f, even1) - lam * (acc2 / _pair_hi(l_f, even1))
    ss = jnp.broadcast_to(jnp.sum(o * o, axis=1, keepdims=True), (8, HEAD_DIM))
    ss = ss + jnp.where(even1, pltpu.roll(ss, 7, axis=0), pltpu.roll(ss, 1, axis=0))
    o = o * lax.rsqrt(ss * (1.0 / DIFF_V_DIM) + RMS_EPS) * gn_ref[...] * (1.0 - LAM0)
    ad_ref[0] = (o * zd_ref[0].astype(F32)).astype(BF16)

    n_cols = ks_sc.shape[0] * 8
    ks = ks_sc[...].reshape(n_cols, HEAD_DIM)
    ks_hi = ks.astype(BF16)
    ks_lo = (ks - ks_hi.astype(F32)).astype(BF16)
    qm8 = qm_ref[0]
    g_all = (lax.dot_general(qm8, ks_hi, _NT, preferred_element_type=F32)
             + lax.dot_general(qm8, ks_lo, _NT, preferred_element_type=F32)) * (1.0 / MOBA_BLOCK)
    g2 = g_all + pltpu.roll(g_all, n_cols - 4, axis=1)
    rw = lax.broadcasted_iota(jnp.int32, (8, n_cols), 0)
    cl = lax.broadcasted_iota(jnp.int32, (8, n_cols), 1)
    clf = cl.astype(F32)
    gm = jnp.where((cl & 7) == (rw >> 1), g2, NEG_INF)
    lane = lax.broadcasted_iota(jnp.int32, (8, HEAD_DIM), 1)
    sel = jnp.zeros((8, HEAD_DIM), jnp.int32)
    for t in range(MOBA_TOPK):
        mx = jnp.max(gm, axis=1, keepdims=True)
        idx = jnp.min(jnp.where(gm == mx, clf, float(n_cols)), axis=1, keepdims=True)
        gm = jnp.where(clf == idx, NEG_INF, gm)
        sel = jnp.where(lane == t, idx.astype(jnp.int32) >> 3, sel)
    sel_ref[0] = sel


def _dec1(page_table, qd8, qm8, kdn8, vdn8, zd8, gn8, lams, ck3, cv3, cmk3, past_len):
    dec_batch, n_pages = page_table.shape
    assert n_pages % (2 * DEC1_PAGES_PER_CHUNK) == 0 and DEC1_PAGES_PER_CHUNK % PAGES_PER_BLOCK == 0
    n_blocks = n_pages // PAGES_PER_BLOCK
    rows = DEC1_PAGES_PER_CHUNK * ROWS_PG
    row8 = lambda: pl.BlockSpec((1, 8, HEAD_DIM), lambda b, pt: (b, 0, 0))
    vec = pl.BlockSpec((1, HEAD_DIM), lambda b, pt: (0, 0))
    hbm = lambda: pl.BlockSpec(memory_space=pl.ANY)
    grid_spec = pltpu.PrefetchScalarGridSpec(
        num_scalar_prefetch=1,
        grid=(dec_batch,),
        in_specs=[row8(), row8(), row8(), row8(), row8(),
                  pl.BlockSpec((8, HEAD_DIM), lambda b, pt: (0, 0)),
                  vec, vec, vec, vec,
                  hbm(), hbm(), hbm()],
        out_specs=(pl.BlockSpec((1, 8, HEAD_DIM), lambda b, pt: (b, 0, 0)),
                   pl.BlockSpec((1, 8, HEAD_DIM), lambda b, pt: (b, 0, 0))),
        scratch_shapes=[pltpu.VMEM((2, rows, HEAD_DIM), F32), pltpu.VMEM((2, rows, HEAD_DIM), F32),
                        pltpu.VMEM((2, rows, HEAD_DIM), F32), pltpu.SemaphoreType.DMA((3, 2)),
                        pltpu.VMEM((n_blocks, 8, HEAD_DIM), F32)],
    )
    return pl.pallas_call(
        functools.partial(_dec1_kernel, n_pages=n_pages, past_len=past_len),
        grid_spec=grid_spec,
        out_shape=(jax.ShapeDtypeStruct((dec_batch, 8, HEAD_DIM), BF16),
                   jax.ShapeDtypeStruct((dec_batch, 8, HEAD_DIM), jnp.int32)),
        compiler_params=pltpu.CompilerParams(
            dimension_semantics=("arbitrary",), vmem_limit_bytes=VMEM_LIMIT_V7X),
        name="dec1",
    )(page_table, qd8, qm8, kdn8, vdn8, zd8, gn8, *lams, ck3, cv3, cmk3)


def _dec2_kernel(pt_ref, sel_ref, qm_ref, kmn_ref, vmn_ref, zm_ref, ck_hbm, cv_hbm, am_ref, kbuf, vbuf, sems, o_sc,
                 *, past_len):
    g = pl.program_id(0)
    n_g = pl.num_programs(0)
    hk = g % MOBA_KV_HEADS
    slot = g % 2
    blk_rows = PAGES_PER_BLOCK * ROWS_PG
    head_rows = MOBA_TOPK * blk_rows
    rows = 2 * head_rows

    def step_copies(gg, sl):
        bb = gg // MOBA_KV_HEADS
        kk = gg % MOBA_KV_HEADS
        out = []
        for gi in range(2):
            for t in range(MOBA_TOPK):
                blk = sel_ref[bb, (2 * kk + gi) * MOBA_TOPK + t]
                for f in range(PAGES_PER_BLOCK):
                    page = pt_ref[bb, blk * PAGES_PER_BLOCK + f]
                    dst = pl.ds(gi * head_rows + (t * PAGES_PER_BLOCK + f) * ROWS_PG, ROWS_PG)
                    out.append(pltpu.make_async_copy(ck_hbm.at[page], kbuf.at[sl, dst, :], sems.at[0, sl]))
                    out.append(pltpu.make_async_copy(cv_hbm.at[page], vbuf.at[sl, dst, :], sems.at[1, sl]))
        return out

    @pl.when(g == 0)
    def _():
        for cp in step_copies(0, 0):
            cp.start()

    @pl.when(g + 1 < n_g)
    def _():
        for cp in step_copies(g + 1, 1 - slot):
            cp.start()

    for cp in step_copies(g, slot):
        cp.wait()

    q8 = qm_ref[0]
    row = lax.broadcasted_iota(jnp.int32, (8, rows), 0)
    col = lax.broadcasted_iota(jnp.int32, (8, rows), 1)
    row1 = lax.broadcasted_iota(jnp.int32, (8, HEAD_DIM), 0)
    b = g // MOBA_KV_HEADS
    blk_col = jnp.zeros((8, rows), jnp.int32)
    for gi in range(2):
        for t in range(MOBA_TOPK):
            lo = gi * head_rows + t * blk_rows
            seg = (col >= lo) & (col < lo + blk_rows)
            blk_col = jnp.where(seg, sel_ref[b, (2 * hk + gi) * MOBA_TOPK + t], blk_col)
    kpos = blk_col * MOBA_BLOCK + ((col & (blk_rows - 1)) >> 2)
    s = lax.dot_general(q8, kbuf[slot].astype(BF16), _NT, preferred_element_type=F32) * SCALE
    s = s - _pow2_neg(row + 1) * (past_len - kpos).astype(F32)
    head_col = 2 * hk + jnp.where(col >= head_rows, 1, 0)
    valid = (row == head_col) & ((col & 3) == hk)
    s = jnp.where(valid, s, NEG_INF)
    s_new = jnp.sum(q8.astype(F32) * kmn_ref[0], axis=1, keepdims=True) * SCALE
    m = jnp.maximum(jnp.max(s, axis=1, keepdims=True), s_new)
    p = jnp.where(valid, jnp.exp(s - m), 0.0)
    p_new = jnp.exp(s_new - m)
    l = jnp.sum(p, axis=1, keepdims=True) + p_new
    o = (jnp.dot(p.astype(BF16), vbuf[slot].astype(BF16), preferred_element_type=F32) + p_new * vmn_ref[0]) / l

    @pl.when(hk == 0)
    def _():
        o_sc[...] = jnp.zeros(o_sc.shape, F32)

    o_sc[...] = jnp.where((row1 >> 1) == hk, o, o_sc[...])

    @pl.when(hk == MOBA_KV_HEADS - 1)
    def _():
        am_ref[0] = (o_sc[...] * zm_ref[0].astype(F32)).astype(BF16)


def _dec2(page_table, sel, qm8, kmn8, vmn8, zm8, cmk3, cmv3, past_len):
    dec_batch = page_table.shape[0]
    rows = 2 * MOBA_TOPK * PAGES_PER_BLOCK * ROWS_PG
    row8 = lambda: pl.BlockSpec((1, 8, HEAD_DIM), lambda g, pt, sl: (g // MOBA_KV_HEADS, 0, 0))
    hbm = lambda: pl.BlockSpec(memory_space=pl.ANY)
    grid_spec = pltpu.PrefetchScalarGridSpec(
        num_scalar_prefetch=2,
        grid=(dec_batch * MOBA_KV_HEADS,),
        in_specs=[row8(), row8(), row8(), row8(), hbm(), hbm()],
        out_specs=pl.BlockSpec((1, 8, HEAD_DIM), lambda g, pt, sl: (g // MOBA_KV_HEADS, 0, 0)),
        scratch_shapes=[pltpu.VMEM((2, rows, HEAD_DIM), F32), pltpu.VMEM((2, rows, HEAD_DIM), F32),
                        pltpu.SemaphoreType.DMA((2, 2)), pltpu.VMEM((8, HEAD_DIM), F32)],
    )
    return pl.pallas_call(
        functools.partial(_dec2_kernel, past_len=past_len),
        grid_spec=grid_spec,
        out_shape=jax.ShapeDtypeStruct((dec_batch, 8, HEAD_DIM), BF16),
        compiler_params=pltpu.CompilerParams(
            dimension_semantics=("arbitrary",), vmem_limit_bytes=VMEM_LIMIT_V7X),
        name="dec2",
    )(page_table, sel, qm8, kmn8, vmn8, zm8, cmk3, cmv3)


def _post_kernel(am_ref, ad_ref, g_ref, x_ref, wmo_ref, wdo_ref, wo_ref, lng_ref, lnb_ref, y_ref):
    ym = jnp.dot(am_ref[...], wmo_ref[...], preferred_element_type=F32)
    yd = jnp.dot(ad_ref[...], wdo_ref[...], preferred_element_type=F32)
    g = g_ref[...].astype(F32)
    merged = g[:, :D_MODEL] * ym + g[:, D_MODEL:] * yd
    f = jnp.dot(merged.astype(BF16), wo_ref[...], preferred_element_type=F32)
    u = ALPHA * x_ref[...] + f
    mu = jnp.mean(u, axis=1, keepdims=True)
    d = u - mu
    var = jnp.mean(d * d, axis=1, keepdims=True)
    y_ref[...] = d * lax.rsqrt(var + LN_EPS) * lng_ref[...] + lnb_ref[...]


def _post(am, ad, gt, x2d, wmo, wdo, wo, lng, lnb, tm):
    m_rows = x2d.shape[0]
    rowblk = lambda w: pl.BlockSpec((tm, w), lambda m: (m, 0))
    full = lambda a: pl.BlockSpec(a.shape, lambda m: (0, 0))
    return pl.pallas_call(
        _post_kernel,
        grid=(m_rows // tm,),
        in_specs=[rowblk(1024), rowblk(1024), rowblk(N_BRANCH * D_MODEL), rowblk(D_MODEL),
                  full(wmo), full(wdo), full(wo), full(lng), full(lnb)],
        out_specs=rowblk(D_MODEL),
        out_shape=jax.ShapeDtypeStruct((m_rows, D_MODEL), F32),
        compiler_params=pltpu.CompilerParams(
            dimension_semantics=("parallel",), vmem_limit_bytes=VMEM_LIMIT_V7X),
        name="post",
    )(am, ad, gt, x2d, wmo, wdo, wo, lng, lnb)


def kernel(x_prompt, x_sample, cache_moba_k, cache_moba_v, cache_diff_k, cache_diff_v, page_table, w_in, w_moba_o, w_diff_o, w_out, lambda_q1, lambda_k1, lambda_q2, lambda_k2, diff_norm_g, ln_g, ln_b):
    assert w_in.shape[0] == DEPTH == 1
    batch, seq, _ = x_prompt.shape
    dec_batch, dec_seq, _ = x_sample.shape
    assert dec_seq == 1 and seq % MOBA_BLOCK == 0
    n_pages = page_table.shape[1]
    past_len = n_pages * PAGE_SIZE
    assert past_len % MOBA_BLOCK == 0 and past_len // MOBA_BLOCK >= MOBA_TOPK
    n_phys = cache_moba_k.shape[1]

    wmo = w_moba_o[0].astype(BF16)
    wdo = w_diff_o[0].astype(BF16)
    wo = w_out[0].astype(BF16)
    lams = (lambda_q1, lambda_k1, lambda_q2, lambda_k2)
    lng = ln_g.reshape(1, D_MODEL)
    lnb = ln_b.reshape(1, D_MODEL)

    xp = x_prompt.reshape(batch * seq, D_MODEL)
    qm, km, vm, zm, qd, kd, vd, zd, gt = _proj(xp, w_in[0], tm=512)
    am = _moba_prompt(qm, km, vm, zm, batch, seq)
    ad = _diff_prompt(qd, kd, vd, zd, diff_norm_g.reshape(DIFF_KV_HEADS, 2, DIFF_V_DIM), lams, batch, seq)
    y_p = _post(am, ad, gt, xp, wmo, wdo, wo, lng, lnb, tm=256)

    xs = x_sample.reshape(dec_batch, D_MODEL)
    qm_s, km_s, vm_s, zm_s, qd_s, kd_s, vd_s, zd_s, gt_s = _proj(xs, w_in[0], tm=dec_batch)
    cmk3 = cache_moba_k.reshape(n_phys, ROWS_PG, HEAD_DIM)
    cmv3 = cache_moba_v.reshape(n_phys, ROWS_PG, HEAD_DIM)

    def diff_pages(c):
        c = c.reshape(n_phys, PAGE_SIZE, DIFF_KV_HEADS, 2, HEAD_DIM)
        return jnp.transpose(c, (0, 1, 3, 2, 4)).reshape(n_phys, ROWS_PG, HEAD_DIM)

    cdk3 = diff_pages(cache_diff_k)
    cdv3 = diff_pages(cache_diff_v)

    def pair_rows(a):
        a = jnp.transpose(a.reshape(dec_batch, 2, DIFF_KV_HEADS, HEAD_DIM), (0, 2, 1, 3))
        a = a.reshape(dec_batch, DIFF_KV_HEADS, 1, 2, HEAD_DIM)
        return jnp.broadcast_to(a, (dec_batch, DIFF_KV_HEADS, 2, 2, HEAD_DIM)).reshape(dec_batch, 8, HEAD_DIM)

    def diff_out(a, *lead):
        a = a.reshape(lead + (2, DIFF_KV_HEADS, HEAD_DIM))
        a = jnp.swapaxes(a, -3, -2)
        return a.reshape((DEPTH,) + lead + (DIFF_KV_HEADS, DIFF_V_DIM))

    qd8 = qd_s.reshape(dec_batch, 8, HEAD_DIM)
    qm8 = qm_s.reshape(dec_batch, 8, HEAD_DIM)
    zd8 = zd_s.reshape(dec_batch, 8, HEAD_DIM)
    zm8 = zm_s.reshape(dec_batch, 8, HEAD_DIM)
    gn8 = diff_norm_g.reshape(8, HEAD_DIM)
    ad_s, sel = _dec1(page_table, qd8, qm8, pair_rows(kd_s), pair_rows(vd_s), zd8, gn8, lams,
                      cdk3, cdv3, cmk3, past_len)
    sel2 = sel[:, :, :MOBA_TOPK].reshape(dec_batch, MOBA_HEADS * MOBA_TOPK)
    kmn8 = jnp.repeat(km_s.reshape(dec_batch, MOBA_KV_HEADS, HEAD_DIM), 2, axis=1)
    vmn8 = jnp.repeat(vm_s.reshape(dec_batch, MOBA_KV_HEADS, HEAD_DIM), 2, axis=1)
    am_s = _dec2(page_table, sel2, qm8, kmn8, vmn8, zm8, cmk3, cmv3, past_len)
    y_s = _post(am_s.reshape(dec_batch, 1024), ad_s.reshape(dec_batch, 1024), gt_s, xs,
                wmo, wdo, wo, lng, lnb, tm=dec_batch)

    return (y_p.reshape(batch, seq, D_MODEL), y_s.reshape(dec_batch, 1, D_MODEL),
            km.reshape(DEPTH, batch, seq, MOBA_KV_HEADS, HEAD_DIM),
            vm.reshape(DEPTH, batch, seq, MOBA_KV_HEADS, HEAD_DIM),
            diff_out(kd, batch, seq), diff_out(vd, batch, seq),
            km_s.reshape(DEPTH, dec_batch, 1, MOBA_KV_HEADS, HEAD_DIM),
            vm_s.reshape(DEPTH, dec_batch, 1, MOBA_KV_HEADS, HEAD_DIM),
            diff_out(kd_s, dec_batch, 1), diff_out(vd_s, dec_batch, 1))
```

```python
import functools
import math

import jax
import jax.numpy as jnp
from jax import lax
from jax.experimental import pallas as pl
from jax.experimental.pallas import tpu as pltpu

F32 = jnp.float32
BF16 = jnp.bfloat16

D_MODEL = 2048
HEAD_DIM = 128
PAGE_SIZE = 128
MOBA_HEADS = 8
MOBA_KV_HEADS = 4
MOBA_BLOCK = 256
MOBA_TOPK = 3
DIFF_HEADS = 4
DIFF_KV_HEADS = 2
DIFF_V_DIM = 2 * HEAD_DIM
N_BRANCH = 2
DEPTH = 1
ALPHA = (2 * DEPTH) ** 0.25
LN_EPS = 1e-5
RMS_EPS = 1e-5
NEG_INF = -1e30
SCALE = HEAD_DIM ** -0.5
LAM0 = 0.8 - 0.6 * math.exp(-0.3 * 0)

VMEM_LIMIT_V7X = 56 * 1024 * 1024

PROJ_TN = 1024
PROJ_TM = 512

_NT = (((1,), (1,)), ((), ()))


def _pow2_neg(k):
    return lax.bitcast_convert_type((127 - k) << 23, F32)


def _silu(v):
    return v * (1.0 / (1.0 + jnp.exp(-v)))


def _sigmoid(v):
    return 1.0 / (1.0 + jnp.exp(-v))


def _proj_kernel(x_ref, w_ref, qm_ref, km_ref, vm_ref, zm_ref, qd_ref, kd_ref, vd_ref, zd_ref, gt_ref, *rest):
    attn_refs, wb_sc = rest[:-1], rest[-1]
    n = pl.program_id(0)
    tm = x_ref.shape[0]
    blk = MOBA_BLOCK

    @pl.when(pl.program_id(1) == 0)
    def _():
        wb_sc[...] = w_ref[...].astype(BF16)

    def tile():
        return jnp.dot(x_ref[...].astype(BF16), wb_sc[...], preferred_element_type=F32)

    half = PROJ_TN // 2

    @pl.when(n == 0)
    def _():
        qm_ref[...] = tile().astype(BF16)

    @pl.when(n == 1)
    def _():
        acc = tile()
        for h in range(MOBA_KV_HEADS):
            rows_h = pl.ds(h, tm, stride=MOBA_KV_HEADS)
            km_ref[rows_h, :] = acc[:, h * HEAD_DIM:(h + 1) * HEAD_DIM]
            vm_ref[rows_h, :] = acc[:, half + h * HEAD_DIM:half + (h + 1) * HEAD_DIM]
        if attn_refs:
            kbm_ref, vtm_ref, kmm_ref = attn_refs[:3]
            for h in range(MOBA_KV_HEADS):
                k_h = acc[:, h * HEAD_DIM:(h + 1) * HEAD_DIM]
                kbm_ref[h] = k_h.astype(BF16)
                kmm_ref[0, h] = jnp.sum(k_h.reshape(tm // blk, blk, HEAD_DIM), axis=1) * (1.0 / blk)
                for r in range(tm // blk):
                    v_hr = acc[r * blk:(r + 1) * blk, half + h * HEAD_DIM:half + (h + 1) * HEAD_DIM]
                    vtm_ref[h, r] = v_hr.T.astype(BF16)

    @pl.when(n == 2)
    def _():
        zm_ref[...] = _silu(tile()).astype(BF16)

    @pl.when(n == 3)
    def _():
        qd_ref[...] = tile().astype(BF16)

    @pl.when(n == 4)
    def _():
        acc = tile()
        for hk in range(DIFF_KV_HEADS):
            for c in range(2):
                rows_c = pl.ds(c * DIFF_KV_HEADS + hk, tm, stride=2 * DIFF_KV_HEADS)
                lo = hk * DIFF_V_DIM + c * HEAD_DIM
                kd_ref[rows_c, :] = acc[:, lo:lo + HEAD_DIM]
                vd_ref[rows_c, :] = acc[:, half + lo:half + lo + HEAD_DIM]
        if attn_refs:
            kbd_ref, vtd_ref = attn_refs[3:]
            for hk in range(DIFF_KV_HEADS):
                kbd_ref[hk] = acc[:, hk * DIFF_V_DIM:(hk + 1) * DIFF_V_DIM].astype(BF16)
                for r in range(tm // blk):
                    v_hr = acc[r * blk:(r + 1) * blk, half + hk * DIFF_V_DIM:half + (hk + 1) * DIFF_V_DIM]
                    vtd_ref[hk, r] = v_hr.T.astype(BF16)

    @pl.when(n == 5)
    def _():
        zd_ref[...] = _silu(tile()).astype(BF16)

    @pl.when(n >= 6)
    def _():
        gt_ref[...] = _sigmoid(tile()).astype(BF16)


def _proj(x2d, w, tm, attn_layouts):
    m_rows = x2d.shape[0]
    n_tiles = w.shape[1] // PROJ_TN
    m_last = m_rows // tm - 1

    def out_map(lo, hi):
        return lambda n, m: (jnp.where(n < lo, 0, jnp.where(n >= hi, m_last, m)), jnp.clip(n - lo, 0, hi - lo - 1))

    def row_tile(lo):
        return lambda n, m: jnp.where(n < lo, 0, jnp.where(n > lo, m_last, m))

    out_shape = (
        jax.ShapeDtypeStruct((m_rows, 1024), BF16),
        jax.ShapeDtypeStruct((m_rows * 4, HEAD_DIM), F32),
        jax.ShapeDtypeStruct((m_rows * 4, HEAD_DIM), F32),
        jax.ShapeDtypeStruct((m_rows, 1024), BF16),
        jax.ShapeDtypeStruct((m_rows, 1024), BF16),
        jax.ShapeDtypeStruct((m_rows * 4, HEAD_DIM), F32),
        jax.ShapeDtypeStruct((m_rows * 4, HEAD_DIM), F32),
        jax.ShapeDtypeStruct((m_rows, 1024), BF16),
        jax.ShapeDtypeStruct((m_rows, 4096), BF16),
    )
    out_specs = (
        pl.BlockSpec((tm, PROJ_TN), out_map(0, 1)),
        pl.BlockSpec((tm * 4, HEAD_DIM), out_map(1, 2)),
        pl.BlockSpec((tm * 4, HEAD_DIM), out_map(1, 2)),
        pl.BlockSpec((tm, PROJ_TN), out_map(2, 3)),
        pl.BlockSpec((tm, PROJ_TN), out_map(3, 4)),
        pl.BlockSpec((tm * 4, HEAD_DIM), out_map(4, 5)),
        pl.BlockSpec((tm * 4, HEAD_DIM), out_map(4, 5)),
        pl.BlockSpec((tm, PROJ_TN), out_map(5, 6)),
        pl.BlockSpec((tm, PROJ_TN), out_map(6, 10)),
    )
    if attn_layouts:
        blk = MOBA_BLOCK
        rt = tm // blk
        tile_m, tile_d = row_tile(1), row_tile(4)
        out_shape += (
            jax.ShapeDtypeStruct((MOBA_KV_HEADS, m_rows, HEAD_DIM), BF16),
            jax.ShapeDtypeStruct((MOBA_KV_HEADS, m_rows // blk, HEAD_DIM, blk), BF16),
            jax.ShapeDtypeStruct((m_rows // tm, MOBA_KV_HEADS, rt, HEAD_DIM), F32),
            jax.ShapeDtypeStruct((DIFF_KV_HEADS, m_rows, DIFF_V_DIM), BF16),
            jax.ShapeDtypeStruct((DIFF_KV_HEADS, m_rows // blk, DIFF_V_DIM, blk), BF16),
        )
        out_specs += (
            pl.BlockSpec((MOBA_KV_HEADS, tm, HEAD_DIM), lambda n, m: (0, tile_m(n, m), 0)),
            pl.BlockSpec((MOBA_KV_HEADS, rt, HEAD_DIM, blk), lambda n, m: (0, tile_m(n, m), 0, 0)),
            pl.BlockSpec((1, MOBA_KV_HEADS, rt, HEAD_DIM), lambda n, m: (tile_m(n, m), 0, 0, 0)),
            pl.BlockSpec((DIFF_KV_HEADS, tm, DIFF_V_DIM), lambda n, m: (0, tile_d(n, m), 0)),
            pl.BlockSpec((DIFF_KV_HEADS, rt, DIFF_V_DIM, blk), lambda n, m: (0, tile_d(n, m), 0, 0)),
        )
    return pl.pallas_call(
        _proj_kernel,
        grid=(n_tiles, m_rows // tm),
        in_specs=[pl.BlockSpec((tm, D_MODEL), lambda n, m: (m, 0)),
                  pl.BlockSpec((D_MODEL, PROJ_TN), lambda n, m: (0, n))],
        out_specs=out_specs,
        out_shape=out_shape,
        scratch_shapes=[pltpu.VMEM((D_MODEL, PROJ_TN), BF16)],
        compiler_params=pltpu.CompilerParams(
            dimension_semantics=("arbitrary", "arbitrary"), vmem_limit_bytes=VMEM_LIMIT_V7X),
        name="proj",
    )(x2d, w)


LOG2E = 1.4426950408889634
SCALE_L2 = SCALE * LOG2E
MASKED = 1e30


def _alibi_tiles_t(blk):
    c_i = lax.broadcasted_iota(jnp.int32, (blk, 2 * blk), 0)
    q_i = lax.broadcasted_iota(jnp.int32, (blk, 2 * blk), 1)
    g = q_i // blk
    rel = (q_i - g * blk - c_i).astype(F32)
    return rel, g[:1], rel >= 0.0


def _slope_row_l2(g_row, head0, slope_mul):
    return _pow2_neg(slope_mul * (head0 + g_row + 1)) * LOG2E


def _lambda_full(lq1_ref, lk1_ref, lq2_ref, lk2_ref):
    a = jnp.sum(lq1_ref[...] * lk1_ref[...], axis=1, keepdims=True)
    b = jnp.sum(lq2_ref[...] * lk2_ref[...], axis=1, keepdims=True)
    return jnp.exp(a) - jnp.exp(b) + LAM0


def _moba_prompt_steps(j, q_ref, kb_ref, vt_ref, km_ref, z_ref, o_ref, bias_sc, acc_sc, st_sc, geom):
    blk = MOBA_BLOCK
    n_kv = MOBA_KV_HEADS
    rel, g_row, causal = geom
    env = {}

    def start():
        qss, ranks = [], []
        for hk in range(n_kv):
            qj = q_ref[0, :, hk * 2 * HEAD_DIM:(hk + 1) * 2 * HEAD_DIM]
            qs = jnp.concatenate([qj[:, :HEAD_DIM], qj[:, HEAD_DIM:]], axis=0)
            kmean = jnp.concatenate([km_ref[t, hk] for t in range(km_ref.shape[0])], axis=0)
            kmean_hi = kmean.astype(BF16)
            kmean_lo = (kmean - kmean_hi.astype(F32)).astype(BF16)
            gate = (lax.dot_general(kmean_hi, qs, _NT, preferred_element_type=F32)
                    + lax.dot_general(kmean_lo, qs, _NT, preferred_element_type=F32))
            sub = lax.broadcasted_iota(jnp.int32, gate.shape, 0)
            gate = jnp.where(sub < j, gate, NEG_INF)
            rank = jnp.zeros(gate.shape, F32)
            for m in range(gate.shape[0]):
                gm = gate[m:m + 1, :]
                ge = jnp.where(gm >= gate, 1.0, 0.0)
                gt = jnp.where(gm > gate, 1.0, 0.0)
                rank = rank + jnp.where(sub > m, ge, gt)
            k_own = kb_ref[hk, pl.ds(pl.multiple_of(j * blk, blk), blk), :]
            u = lax.dot_general(k_own, qs, _NT, preferred_element_type=F32) * SCALE_L2 - bias_sc[hk]
            u = jnp.where(causal, u, -MASKED)
            m0 = jnp.max(u, axis=0, keepdims=True)
            p = jnp.exp2(u - m0)
            st_sc[2 * hk:2 * hk + 1, :] = m0
            st_sc[2 * hk + 1:2 * hk + 2, :] = jnp.sum(p, axis=0, keepdims=True)
            acc_sc[hk] = jnp.dot(vt_ref[hk, j], p.astype(BF16), preferred_element_type=F32)
            qss.append(qs)
            ranks.append(rank)
        env["qss"], env["ranks"] = qss, ranks

    def piece(n):
        dist_blocks = (j - n).astype(F32)
        for hk in range(n_kv):
            m_run = st_sc[2 * hk:2 * hk + 1, :]
            l_run = st_sc[2 * hk + 1:2 * hk + 2, :]
            k_n = kb_ref[hk, pl.ds(pl.multiple_of(n * blk, blk), blk), :]
            u = lax.dot_general(k_n, env["qss"][hk], _NT, preferred_element_type=F32) * SCALE_L2 - bias_sc[hk]
            rank = env["ranks"][hk]
            sub = lax.broadcasted_iota(jnp.int32, rank.shape, 0)
            rk = jnp.sum(jnp.where(sub == n, rank, 0.0), axis=0, keepdims=True)
            slope_blk = _slope_row_l2(g_row, 2 * hk, 1) * float(blk)
            rowc = jnp.where(rk < float(MOBA_TOPK), slope_blk * dist_blocks, MASKED)
            m_new = jnp.maximum(m_run, jnp.max(u, axis=0, keepdims=True) - rowc)
            alpha = jnp.exp2(m_run - m_new)
            p = jnp.exp2(u - (rowc + m_new))
            st_sc[2 * hk:2 * hk + 1, :] = m_new
            st_sc[2 * hk + 1:2 * hk + 2, :] = alpha * l_run + jnp.sum(p, axis=0, keepdims=True)
            acc_sc[hk] = alpha * acc_sc[hk] + jnp.dot(vt_ref[hk, n], p.astype(BF16), preferred_element_type=F32)

    def finish():
        for hk in range(n_kv):
            ot = acc_sc[hk] / st_sc[2 * hk + 1:2 * hk + 2, :]
            o2 = jnp.concatenate([ot[:, :blk].T, ot[:, blk:].T], axis=1)
            cols = slice(hk * 2 * HEAD_DIM, (hk + 1) * 2 * HEAD_DIM)
            o_ref[0, :, cols] = (o2 * z_ref[0, :, cols].astype(F32)).astype(BF16)

    return start, piece, finish


def _diff_prompt_steps(j, lam, q_ref, kb_ref, vt_ref, z_ref, gn_ref, o_ref, bias_sc, acc_sc, st_sc, geom):
    blk = MOBA_BLOCK
    n_kv = DIFF_KV_HEADS
    rel, g_row, causal = geom
    chains = [(hk, mi) for hk in range(n_kv) for mi in range(2)]
    env = {}

    def scores(hk, mi, n):
        k_n = kb_ref[hk, pl.ds(pl.multiple_of(n * blk, blk), blk), mi * HEAD_DIM:(mi + 1) * HEAD_DIM]
        return lax.dot_general(k_n, env["qss"][hk, mi], _NT, preferred_element_type=F32) * SCALE_L2 - bias_sc[hk]

    def start():
        qss = {}
        for hk in range(n_kv):
            qj = q_ref[0, :, hk * 4 * HEAD_DIM:(hk + 1) * 4 * HEAD_DIM]
            for mi in range(2):
                qss[hk, mi] = jnp.concatenate([qj[:, mi * HEAD_DIM:(mi + 1) * HEAD_DIM],
                                               qj[:, (2 + mi) * HEAD_DIM:(3 + mi) * HEAD_DIM]], axis=0)
        env["qss"] = qss
        for ci, (hk, mi) in enumerate(chains):
            u = jnp.where(causal, scores(hk, mi, j), -MASKED)
            m0 = jnp.max(u, axis=0, keepdims=True)
            p = jnp.exp2(u - m0)
            st_sc[2 * ci:2 * ci + 1, :] = m0
            st_sc[2 * ci + 1:2 * ci + 2, :] = jnp.sum(p, axis=0, keepdims=True)
            acc_sc[ci] = jnp.dot(vt_ref[hk, j], p.astype(BF16), preferred_element_type=F32)

    def piece(n):
        dist_blocks = (j - n).astype(F32)
        for ci, (hk, mi) in enumerate(chains):
            m_run = st_sc[2 * ci:2 * ci + 1, :]
            l_run = st_sc[2 * ci + 1:2 * ci + 2, :]
            rowc = _slope_row_l2(g_row, 2 * hk, 2) * float(blk) * dist_blocks
            u = scores(hk, mi, n)
            m_new = jnp.maximum(m_run, jnp.max(u, axis=0, keepdims=True) - rowc)
            alpha = jnp.exp2(m_run - m_new)
            p = jnp.exp2(u - (rowc + m_new))
            st_sc[2 * ci:2 * ci + 1, :] = m_new
            st_sc[2 * ci + 1:2 * ci + 2, :] = alpha * l_run + jnp.sum(p, axis=0, keepdims=True)
            acc_sc[ci] = alpha * acc_sc[ci] + jnp.dot(vt_ref[hk, n], p.astype(BF16), preferred_element_type=F32)

    def finish():
        for hk in range(n_kv):
            c1, c2 = 2 * hk, 2 * hk + 1
            ot = (acc_sc[c1] / st_sc[2 * c1 + 1:2 * c1 + 2, :]
                  - lam * (acc_sc[c2] / st_sc[2 * c2 + 1:2 * c2 + 2, :]))
            ms = jnp.mean(ot * ot, axis=0, keepdims=True)
            ot = ot * lax.rsqrt(ms + RMS_EPS)
            gn = gn_ref[hk]
            o2 = jnp.concatenate([ot[:, :blk].T * gn[0:1, :], ot[:, blk:].T * gn[1:2, :]], axis=1) * (1.0 - LAM0)
            cols = slice(hk * 2 * DIFF_V_DIM, (hk + 1) * 2 * DIFF_V_DIM)
            o_ref[0, :, cols] = (o2 * z_ref[0, :, cols].astype(F32)).astype(BF16)

    return start, piece, finish


ROWS_PG = PAGE_SIZE * 4
DEC1_PAGES_PER_CHUNK = 8
PAGES_PER_BLOCK = MOBA_BLOCK // PAGE_SIZE


def _pair_lo(x, even):
    return jnp.where(even, x, pltpu.roll(x, 1, axis=0))


def _pair_hi(x, even):
    return jnp.where(even, pltpu.roll(x, x.shape[0] - 1, axis=0), x)


def _dec1_attn_kernel(pt_ref, qd_ref, qm_ref, kdn_ref, vdn_ref, zd_ref, gn_ref, lq1_ref, lk1_ref, lq2_ref, lk2_ref,
                      ck_hbm, cv_hbm, cmk_hbm,
                      qmp_ref, kbm_ref, vtm_ref, kmm_ref, zmp_ref, qdp_ref, kbd_ref, vtd_ref, zdp_ref, gnp_ref,
                      ad_ref, sel_ref, amp_ref, adp_ref,
                      kbuf, vbuf, mbuf, sems, ks_sc, bias_m, bias_d, acc_m, acc_d, st_m, st_d,
                      *, n_pages, past_len, n_blk):
    b = pl.program_id(0)
    n_b = pl.num_programs(0)
    j = b % n_blk
    ppc = DEC1_PAGES_PER_CHUNK
    n_chunks = n_pages // ppc
    rows = ppc * ROWS_PG
    streams = ((ck_hbm, kbuf), (cv_hbm, vbuf), (cmk_hbm, mbuf))

    def chunk_copies(bb, c, slot):
        out = []
        for k in range(ppc):
            page = pt_ref[bb, c * ppc + k]
            for si, (src, dst) in enumerate(streams):
                out.append(pltpu.make_async_copy(
                    src.at[page], dst.at[slot, pl.ds(k * ROWS_PG, ROWS_PG), :], sems.at[si, slot]))
        return out

    geom = _alibi_tiles_t(MOBA_BLOCK)

    @pl.when(b == 0)
    def _():
        for cp in chunk_copies(0, 0, 0) + chunk_copies(0, 1, 1):
            cp.start()
        for hk in range(MOBA_KV_HEADS):
            bias_m[hk] = geom[0] * _slope_row_l2(geom[1], 2 * hk, 1)
        for hk in range(DIFF_KV_HEADS):
            bias_d[hk] = geom[0] * _slope_row_l2(geom[1], 2 * hk, 2)

    lam = _lambda_full(lq1_ref, lk1_ref, lq2_ref, lk2_ref)
    m_start, m_piece, m_finish = _moba_prompt_steps(j, qmp_ref, kbm_ref, vtm_ref, kmm_ref, zmp_ref, amp_ref,
                                                    bias_m, acc_m, st_m, geom)
    d_start, d_piece, d_finish = _diff_prompt_steps(j, lam, qdp_ref, kbd_ref, vtd_ref, zdp_ref, gnp_ref, adp_ref,
                                                    bias_d, acc_d, st_d, geom)
    m_start()
    d_start()

    q8 = qd_ref[0]
    row = lax.broadcasted_iota(jnp.int32, (8, rows), 0)
    col = lax.broadcasted_iota(jnp.int32, (8, rows), 1)
    row1 = lax.broadcasted_iota(jnp.int32, (8, HEAD_DIM), 0)
    even = (row & 1) == 0
    even1 = (row1 & 1) == 0
    valid = (col & 3) == ((row & 1) * 2 + (row >> 2))
    slope = _pow2_neg(2 * ((row >> 1) + 1))
    posf = (col >> 2).astype(F32)

    def chunk_compute(c, slot, carry):
        m_old, l_old, acc = carry
        s = lax.dot_general(q8, kbuf[slot].astype(BF16), _NT, preferred_element_type=F32) * SCALE
        base = jnp.asarray(past_len - c * (ppc * PAGE_SIZE)).astype(F32)
        s = jnp.where(valid, s - slope * (base - posf), NEG_INF)
        m_new = jnp.maximum(m_old, jnp.max(s, axis=1, keepdims=True))
        alpha = jnp.exp(m_old - m_new)
        p = jnp.where(valid, jnp.exp(s - m_new[:, :1]), 0.0)
        l_new = alpha * l_old + jnp.sum(p, axis=1, keepdims=True)
        p_r = pltpu.roll(p, 2, axis=1)
        p_l = pltpu.roll(p, rows - 2, axis=1)
        w1 = jnp.where(even, p, pltpu.roll(p_r, 1, axis=0))
        w2 = jnp.where(even, pltpu.roll(p_l, 7, axis=0), p)
        w = jnp.concatenate([w1, w2], axis=0).astype(BF16)
        a16 = jnp.concatenate([_pair_lo(alpha, even1), _pair_hi(alpha, even1)], axis=0)
        acc = a16 * acc + jnp.dot(w, vbuf[slot].astype(BF16), preferred_element_type=F32)
        bpc = ppc // PAGES_PER_BLOCK
        ksum = jnp.sum(mbuf[slot].reshape(bpc, PAGES_PER_BLOCK * ROWS_PG // 8, 8, HEAD_DIM), axis=1)
        ks_sc[pl.ds(c * bpc, bpc)] = ksum
        return m_new, l_new, acc

    def pair_body(c2, carry):
        for u in range(2):
            c = 2 * c2 + u
            for cp in chunk_copies(b, c, u):
                cp.wait()
            carry = chunk_compute(c, u, carry)
            in_b = c + 2 < n_chunks

            @pl.when(in_b | (b + 1 < n_b))
            def _():
                for cp in chunk_copies(jnp.where(in_b, b, b + 1), jnp.where(in_b, c + 2, c + 2 - n_chunks), u):
                    cp.start()

            pl.when(c2 < j)(functools.partial(m_piece if u == 0 else d_piece, c2))
        return carry

    init = (jnp.full((8, HEAD_DIM), NEG_INF, F32), jnp.zeros((8, HEAD_DIM), F32), jnp.zeros((16, HEAD_DIM), F32))
    m_o, l_o, acc = lax.fori_loop(0, n_chunks // 2, pair_body, init)
    m_finish()
    d_finish()

    qf = q8.astype(F32)
    s_new = jnp.sum(qf * kdn_ref[0], axis=1, keepdims=True) * SCALE
    m_f = jnp.maximum(m_o, s_new)
    a_f = jnp.exp(m_o - m_f)
    p_new = jnp.exp(s_new - m_f)
    l_f = a_f * l_o + p_new
    vdn = vdn_ref[0]
    acc1 = _pair_lo(a_f, even1) * acc[:8] + _pair_lo(p_new, even1) * vdn
    acc2 = _pair_hi(a_f, even1) * acc[8:] + _pair_hi(p_new, even1) * vdn
    o = acc1 / _pair_lo(l_f, even1) - lam * (acc2 / _pair_hi(l_f, even1))
    ss = jnp.broadcast_to(jnp.sum(o * o, axis=1, keepdims=True), (8, HEAD_DIM))
    ss = ss + jnp.where(even1, pltpu.roll(ss, 7, axis=0), pltpu.roll(ss, 1, axis=0))
    o = o * lax.rsqrt(ss * (1.0 / DIFF_V_DIM) + RMS_EPS) * gn_ref[...] * (1.0 - LAM0)
    ad_ref[0] = (o * zd_ref[0].astype(F32)).astype(BF16)

    n_cols = ks_sc.shape[0] * 8
    ks = ks_sc[...].reshape(n_cols, HEAD_DIM)
    ks_hi = ks.astype(BF16)
    ks_lo = (ks - ks_hi.astype(F32)).astype(BF16)
    qm8 = qm_ref[0]
    g_all = (lax.dot_general(qm8, ks_hi, _NT, preferred_element_type=F32)
             + lax.dot_general(qm8, ks_lo, _NT, preferred_element_type=F32)) * (1.0 / MOBA_BLOCK)
    g2 = g_all + pltpu.roll(g_all, n_cols - 4, axis=1)
    rw = lax.broadcasted_iota(jnp.int32, (8, n_cols), 0)
    cl = lax.broadcasted_iota(jnp.int32, (8, n_cols), 1)
    clf = cl.astype(F32)
    gm = jnp.where((cl & 7) == (rw >> 1), g2, NEG_INF)
    lane = lax.broadcasted_iota(jnp.int32, (8, HEAD_DIM), 1)
    sel = jnp.zeros((8, HEAD_DIM), jnp.int32)
    for t in range(MOBA_TOPK):
        mx = jnp.max(gm, axis=1, keepdims=True)
        idx = jnp.min(jnp.where(gm == mx, clf, float(n_cols)), axis=1, keepdims=True)
        gm = jnp.where(clf == idx, NEG_INF, gm)
        sel = jnp.where(lane == t, idx.astype(jnp.int32) >> 3, sel)
    sel_ref[0] = sel


def _dec1_attn(page_table, qd8, qm8, kdn8, vdn8, zd8, gn8, lams, ck3, cv3, cmk3, prompt_ops, past_len, batch, seq):
    dec_batch, n_pages = page_table.shape
    blk = MOBA_BLOCK
    n_blk = seq // blk
    n_chunks = n_pages // DEC1_PAGES_PER_CHUNK
    assert dec_batch == batch * n_blk and n_chunks == 2 * n_blk and DEC1_PAGES_PER_CHUNK % PAGES_PER_BLOCK == 0
    n_blocks = n_pages // PAGES_PER_BLOCK
    rows = DEC1_PAGES_PER_CHUNK * ROWS_PG
    rt = PROJ_TM // blk
    row8 = lambda: pl.BlockSpec((1, 8, HEAD_DIM), lambda b, pt: (b, 0, 0))
    vec = pl.BlockSpec((1, HEAD_DIM), lambda b, pt: (0, 0))
    hbm = lambda: pl.BlockSpec(memory_space=pl.ANY)
    qblk = lambda: pl.BlockSpec((1, blk, 1024), lambda b, pt: (b // n_blk, b % n_blk, 0))
    grid_spec = pltpu.PrefetchScalarGridSpec(
        num_scalar_prefetch=1,
        grid=(dec_batch,),
        in_specs=[row8(), row8(), row8(), row8(), row8(),
                  pl.BlockSpec((8, HEAD_DIM), lambda b, pt: (0, 0)),
                  vec, vec, vec, vec,
                  hbm(), hbm(), hbm(),
                  qblk(),
                  pl.BlockSpec((MOBA_KV_HEADS, seq, HEAD_DIM), lambda b, pt: (0, b // n_blk, 0)),
                  pl.BlockSpec((MOBA_KV_HEADS, n_blk, HEAD_DIM, blk), lambda b, pt: (0, b // n_blk, 0, 0)),
                  pl.BlockSpec((n_blk // rt, MOBA_KV_HEADS, rt, HEAD_DIM), lambda b, pt: (b // n_blk, 0, 0, 0)),
                  qblk(),
                  qblk(),
                  pl.BlockSpec((DIFF_KV_HEADS, seq, DIFF_V_DIM), lambda b, pt: (0, b // n_blk, 0)),
                  pl.BlockSpec((DIFF_KV_HEADS, n_blk, DIFF_V_DIM, blk), lambda b, pt: (0, b // n_blk, 0, 0)),
                  qblk(),
                  pl.BlockSpec((DIFF_KV_HEADS, 2, DIFF_V_DIM), lambda b, pt: (0, 0, 0))],
        out_specs=(pl.BlockSpec((1, 8, HEAD_DIM), lambda b, pt: (b, 0, 0)),
                   pl.BlockSpec((1, 8, HEAD_DIM), lambda b, pt: (b, 0, 0)),
                   qblk(), qblk()),
        scratch_shapes=[pltpu.VMEM((2, rows, HEAD_DIM), F32), pltpu.VMEM((2, rows, HEAD_DIM), F32),
                        pltpu.VMEM((2, rows, HEAD_DIM), F32), pltpu.SemaphoreType.DMA((3, 2)),
                        pltpu.VMEM((n_blocks, 8, HEAD_DIM), F32),
                        pltpu.VMEM((MOBA_KV_HEADS, blk, 2 * blk), F32),
                        pltpu.VMEM((DIFF_KV_HEADS, blk, 2 * blk), F32),
                        pltpu.VMEM((MOBA_KV_HEADS, HEAD_DIM, 2 * blk), F32),
                        pltpu.VMEM((2 * DIFF_KV_HEADS, DIFF_V_DIM, 2 * blk), F32),
                        pltpu.VMEM((2 * MOBA_KV_HEADS, 2 * blk), F32),
                        pltpu.VMEM((4 * DIFF_KV_HEADS, 2 * blk), F32)],
    )
    return pl.pallas_call(
        functools.partial(_dec1_attn_kernel, n_pages=n_pages, past_len=past_len, n_blk=n_blk),
        grid_spec=grid_spec,
        out_shape=(jax.ShapeDtypeStruct((dec_batch, 8, HEAD_DIM), BF16),
                   jax.ShapeDtypeStruct((dec_batch, 8, HEAD_DIM), jnp.int32),
                   jax.ShapeDtypeStruct((batch, seq, 1024), BF16),
                   jax.ShapeDtypeStruct((batch, seq, 1024), BF16)),
        compiler_params=pltpu.CompilerParams(
            dimension_semantics=("arbitrary",), vmem_limit_bytes=VMEM_LIMIT_V7X),
        name="dec1_attn",
    )(page_table, qd8, qm8, kdn8, vdn8, zd8, gn8, *lams, ck3, cv3, cmk3, *prompt_ops)


def _dec2_kernel(pt_ref, sel_ref, qm_ref, kmn_ref, vmn_ref, zm_ref, ck_hbm, cv_hbm, am_ref, kbuf, vbuf, sems, o_sc,
                 *, past_len):
    g = pl.program_id(0)
    n_g = pl.num_programs(0)
    hk = g % MOBA_KV_HEADS
    slot = g % 2
    blk_rows = PAGES_PER_BLOCK * ROWS_PG
    head_rows = MOBA_TOPK * blk_rows
    rows = 2 * head_rows

    def step_copies(gg, sl):
        bb = gg // MOBA_KV_HEADS
        kk = gg % MOBA_KV_HEADS
        out = []
        for gi in range(2):
            for t in range(MOBA_TOPK):
                blk = sel_ref[bb, (2 * kk + gi) * MOBA_TOPK + t]
                for f in range(PAGES_PER_BLOCK):
                    page = pt_ref[bb, blk * PAGES_PER_BLOCK + f]
                    dst = pl.ds(gi * head_rows + (t * PAGES_PER_BLOCK + f) * ROWS_PG, ROWS_PG)
                    out.append(pltpu.make_async_copy(ck_hbm.at[page], kbuf.at[sl, dst, :], sems.at[0, sl]))
                    out.append(pltpu.make_async_copy(cv_hbm.at[page], vbuf.at[sl, dst, :], sems.at[1, sl]))
        return out

    @pl.when(g == 0)
    def _():
        for cp in step_copies(0, 0):
            cp.start()

    @pl.when(g + 1 < n_g)
    def _():
        for cp in step_copies(g + 1, 1 - slot):
            cp.start()

    for cp in step_copies(g, slot):
        cp.wait()

    q8 = qm_ref[0]
    row = lax.broadcasted_iota(jnp.int32, (8, rows), 0)
    col = lax.broadcasted_iota(jnp.int32, (8, rows), 1)
    row1 = lax.broadcasted_iota(jnp.int32, (8, HEAD_DIM), 0)
    b = g // MOBA_KV_HEADS
    blk_col = jnp.zeros((8, rows), jnp.int32)
    for gi in range(2):
        for t in range(MOBA_TOPK):
            lo = gi * head_rows + t * blk_rows
            seg = (col >= lo) & (col < lo + blk_rows)
            blk_col = jnp.where(seg, sel_ref[b, (2 * hk + gi) * MOBA_TOPK + t], blk_col)
    kpos = blk_col * MOBA_BLOCK + ((col & (blk_rows - 1)) >> 2)
    s = lax.dot_general(q8, kbuf[slot].astype(BF16), _NT, preferred_element_type=F32) * SCALE
    s = s - _pow2_neg(row + 1) * (past_len - kpos).astype(F32)
    head_col = 2 * hk + jnp.where(col >= head_rows, 1, 0)
    valid = (row == head_col) & ((col & 3) == hk)
    s = jnp.where(valid, s, NEG_INF)
    s_new = jnp.sum(q8.astype(F32) * kmn_ref[0], axis=1, keepdims=True) * SCALE
    m = jnp.maximum(jnp.max(s, axis=1, keepdims=True), s_new)
    p = jnp.where(valid, jnp.exp(s - m), 0.0)
    p_new = jnp.exp(s_new - m)
    l = jnp.sum(p, axis=1, keepdims=True) + p_new
    o = (jnp.dot(p.astype(BF16), vbuf[slot].astype(BF16), preferred_element_type=F32) + p_new * vmn_ref[0]) / l

    @pl.when(hk == 0)
    def _():
        o_sc[...] = jnp.zeros(o_sc.shape, F32)

    o_sc[...] = jnp.where((row1 >> 1) == hk, o, o_sc[...])

    @pl.when(hk == MOBA_KV_HEADS - 1)
    def _():
        am_ref[0] = (o_sc[...] * zm_ref[0].astype(F32)).astype(BF16)


def _dec2(page_table, sel, qm8, kmn8, vmn8, zm8, cmk3, cmv3, past_len):
    dec_batch = page_table.shape[0]
    rows = 2 * MOBA_TOPK * PAGES_PER_BLOCK * ROWS_PG
    row8 = lambda: pl.BlockSpec((1, 8, HEAD_DIM), lambda g, pt, sl: (g // MOBA_KV_HEADS, 0, 0))
    hbm = lambda: pl.BlockSpec(memory_space=pl.ANY)
    grid_spec = pltpu.PrefetchScalarGridSpec(
        num_scalar_prefetch=2,
        grid=(dec_batch * MOBA_KV_HEADS,),
        in_specs=[row8(), row8(), row8(), row8(), hbm(), hbm()],
        out_specs=pl.BlockSpec((1, 8, HEAD_DIM), lambda g, pt, sl: (g // MOBA_KV_HEADS, 0, 0)),
        scratch_shapes=[pltpu.VMEM((2, rows, HEAD_DIM), F32), pltpu.VMEM((2, rows, HEAD_DIM), F32),
                        pltpu.SemaphoreType.DMA((2, 2)), pltpu.VMEM((8, HEAD_DIM), F32)],
    )
    return pl.pallas_call(
        functools.partial(_dec2_kernel, past_len=past_len),
        grid_spec=grid_spec,
        out_shape=jax.ShapeDtypeStruct((dec_batch, 8, HEAD_DIM), BF16),
        compiler_params=pltpu.CompilerParams(
            dimension_semantics=("arbitrary",), vmem_limit_bytes=VMEM_LIMIT_V7X),
        name="dec2",
    )(page_table, sel, qm8, kmn8, vmn8, zm8, cmk3, cmv3)


def _post_kernel(am_ref, ad_ref, g_ref, x_ref, wmo_ref, wdo_ref, wo_ref, lng_ref, lnb_ref, y_ref):
    ym = jnp.dot(am_ref[...], wmo_ref[...], preferred_element_type=F32)
    yd = jnp.dot(ad_ref[...], wdo_ref[...], preferred_element_type=F32)
    g = g_ref[...].astype(F32)
    merged = g[:, :D_MODEL] * ym + g[:, D_MODEL:] * yd
    f = jnp.dot(merged.astype(BF16), wo_ref[...], preferred_element_type=F32)
    u = ALPHA * x_ref[...] + f
    mu = jnp.mean(u, axis=1, keepdims=True)
    d = u - mu
    var = jnp.mean(d * d, axis=1, keepdims=True)
    y_ref[...] = d * lax.rsqrt(var + LN_EPS) * lng_ref[...] + lnb_ref[...]


def _post(am, ad, gt, x2d, wmo, wdo, wo, lng, lnb, tm):
    m_rows = x2d.shape[0]
    rowblk = lambda w: pl.BlockSpec((tm, w), lambda m: (m, 0))
    full = lambda a: pl.BlockSpec(a.shape, lambda m: (0, 0))
    return pl.pallas_call(
        _post_kernel,
        grid=(m_rows // tm,),
        in_specs=[rowblk(1024), rowblk(1024), rowblk(N_BRANCH * D_MODEL), rowblk(D_MODEL),
                  full(wmo), full(wdo), full(wo), full(lng), full(lnb)],
        out_specs=rowblk(D_MODEL),
        out_shape=jax.ShapeDtypeStruct((m_rows, D_MODEL), F32),
        compiler_params=pltpu.CompilerParams(
            dimension_semantics=("parallel",), vmem_limit_bytes=VMEM_LIMIT_V7X),
        name="post",
    )(am, ad, gt, x2d, wmo, wdo, wo, lng, lnb)


def kernel(x_prompt, x_sample, cache_moba_k, cache_moba_v, cache_diff_k, cache_diff_v, page_table, w_in, w_moba_o, w_diff_o, w_out, lambda_q1, lambda_k1, lambda_q2, lambda_k2, diff_norm_g, ln_g, ln_b):
    assert w_in.shape[0] == DEPTH == 1
    batch, seq, _ = x_prompt.shape
    dec_batch, dec_seq, _ = x_sample.shape
    assert dec_seq == 1 and seq % MOBA_BLOCK == 0
    n_pages = page_table.shape[1]
    past_len = n_pages * PAGE_SIZE
    assert past_len % MOBA_BLOCK == 0 and past_len // MOBA_BLOCK >= MOBA_TOPK
    n_phys = cache_moba_k.shape[1]

    wmo = w_moba_o[0].astype(BF16)
    wdo = w_diff_o[0].astype(BF16)
    wo = w_out[0].astype(BF16)
    lams = (lambda_q1, lambda_k1, lambda_q2, lambda_k2)
    lng = ln_g.reshape(1, D_MODEL)
    lnb = ln_b.reshape(1, D_MODEL)

    xp = x_prompt.reshape(batch * seq, D_MODEL)
    qm, km, vm, zm, qd, kd, vd, zd, gt, kbm, vtm, kmm, kbd, vtd = _proj(xp, w_in[0], tm=PROJ_TM, attn_layouts=True)
    xs = x_sample.reshape(dec_batch, D_MODEL)
    qm_s, km_s, vm_s, zm_s, qd_s, kd_s, vd_s, zd_s, gt_s = _proj(xs, w_in[0], tm=dec_batch, attn_layouts=False)
    cmk3 = cache_moba_k.reshape(n_phys, ROWS_PG, HEAD_DIM)
    cmv3 = cache_moba_v.reshape(n_phys, ROWS_PG, HEAD_DIM)

    def diff_pages(c):
        c = c.reshape(n_phys, PAGE_SIZE, DIFF_KV_HEADS, 2, HEAD_DIM)
        return jnp.transpose(c, (0, 1, 3, 2, 4)).reshape(n_phys, ROWS_PG, HEAD_DIM)

    cdk3 = diff_pages(cache_diff_k)
    cdv3 = diff_pages(cache_diff_v)

    def pair_rows(a):
        a = jnp.transpose(a.reshape(dec_batch, 2, DIFF_KV_HEADS, HEAD_DIM), (0, 2, 1, 3))
        a = a.reshape(dec_batch, DIFF_KV_HEADS, 1, 2, HEAD_DIM)
        return jnp.broadcast_to(a, (dec_batch, DIFF_KV_HEADS, 2, 2, HEAD_DIM)).reshape(dec_batch, 8, HEAD_DIM)

    def diff_out(a, *lead):
        a = a.reshape(lead + (2, DIFF_KV_HEADS, HEAD_DIM))
        a = jnp.swapaxes(a, -3, -2)
        return a.reshape((DEPTH,) + lead + (DIFF_KV_HEADS, DIFF_V_DIM))

    qd8 = qd_s.reshape(dec_batch, 8, HEAD_DIM)
    qm8 = qm_s.reshape(dec_batch, 8, HEAD_DIM)
    zd8 = zd_s.reshape(dec_batch, 8, HEAD_DIM)
    zm8 = zm_s.reshape(dec_batch, 8, HEAD_DIM)
    gn8 = diff_norm_g.reshape(8, HEAD_DIM)
    prompt_ops = (qm.reshape(batch, seq, 1024), kbm, vtm, kmm, zm.reshape(batch, seq, 1024),
                  qd.reshape(batch, seq, 1024), kbd, vtd, zd.reshape(batch, seq, 1024),
                  diff_norm_g.reshape(DIFF_KV_HEADS, 2, DIFF_V_DIM))
    ad_s, sel, am, ad = _dec1_attn(page_table, qd8, qm8, pair_rows(kd_s), pair_rows(vd_s), zd8, gn8, lams,
                                   cdk3, cdv3, cmk3, prompt_ops, past_len, batch, seq)
    y_p = _post(am.reshape(batch * seq, 1024), ad.reshape(batch * seq, 1024), gt, xp, wmo, wdo, wo, lng, lnb, tm=256)
    sel2 = sel[:, :, :MOBA_TOPK].reshape(dec_batch, MOBA_HEADS * MOBA_TOPK)
    kmn8 = jnp.repeat(km_s.reshape(dec_batch, MOBA_KV_HEADS, HEAD_DIM), 2, axis=1)
    vmn8 = jnp.repeat(vm_s.reshape(dec_batch, MOBA_KV_HEADS, HEAD_DIM), 2, axis=1)
    am_s = _dec2(page_table, sel2, qm8, kmn8, vmn8, zm8, cmk3, cmv3, past_len)
    y_s = _post(am_s.reshape(dec_batch, 1024), ad_s.reshape(dec_batch, 1024), gt_s, xs,
                wmo, wdo, wo, lng, lnb, tm=dec_batch)

    return (y_p.reshape(batch, seq, D_MODEL), y_s.reshape(dec_batch, 1, D_MODEL),
            km.reshape(DEPTH, batch, seq, MOBA_KV_HEADS, HEAD_DIM),
            vm.reshape(DEPTH, batch, seq, MOBA_KV_HEADS, HEAD_DIM),
            diff_out(kd, batch, seq), diff_out(vd, batch, seq),
            km_s.reshape(DEPTH, dec_batch, 1, MOBA_KV_HEADS, HEAD_DIM),
            vm_s.reshape(DEPTH, dec_batch, 1, MOBA_KV_HEADS, HEAD_DIM),
            diff_out(kd_s, dec_batch, 1), diff_out(vd_s, dec_batch, 1))
```

```python
import functools
import math

import jax
import jax.numpy as jnp
from jax import lax
from jax.experimental import pallas as pl
from jax.experimental.pallas import tpu as pltpu

F32 = jnp.float32
BF16 = jnp.bfloat16

D_MODEL = 2048
HEAD_DIM = 128
PAGE_SIZE = 128
MOBA_HEADS = 8
MOBA_KV_HEADS = 4
MOBA_BLOCK = 256
MOBA_TOPK = 3
DIFF_HEADS = 4
DIFF_KV_HEADS = 2
DIFF_V_DIM = 2 * HEAD_DIM
N_BRANCH = 2
DEPTH = 1
ALPHA = (2 * DEPTH) ** 0.25
LN_EPS = 1e-5
RMS_EPS = 1e-5
NEG_INF = -1e30
SCALE = HEAD_DIM ** -0.5
LAM0 = 0.8 - 0.6 * math.exp(-0.3 * 0)

VMEM_LIMIT_V7X = 56 * 1024 * 1024

PROJ_TN = 1024
PROJ_TM = 512

_NT = (((1,), (1,)), ((), ()))


def _pow2_neg(k):
    return lax.bitcast_convert_type((127 - k) << 23, F32)


def _silu(v):
    return v * (1.0 / (1.0 + jnp.exp(-v)))


def _sigmoid(v):
    return 1.0 / (1.0 + jnp.exp(-v))


def _proj_kernel(x_ref, w_ref, qm_ref, km_ref, vm_ref, zm_ref, qd_ref, kd_ref, vd_ref, zd_ref, gt_ref, *rest):
    attn_refs, wb_sc = rest[:-1], rest[-1]
    n = pl.program_id(0)
    tm = x_ref.shape[0]
    blk = MOBA_BLOCK

    @pl.when(pl.program_id(1) == 0)
    def _():
        wb_sc[...] = w_ref[...].astype(BF16)

    def tile():
        return jnp.dot(x_ref[...].astype(BF16), wb_sc[...], preferred_element_type=F32)

    half = PROJ_TN // 2

    @pl.when(n == 0)
    def _():
        qm_ref[...] = tile().astype(BF16)

    @pl.when(n == 1)
    def _():
        acc = tile()
        for h in range(MOBA_KV_HEADS):
            rows_h = pl.ds(h, tm, stride=MOBA_KV_HEADS)
            km_ref[rows_h, :] = acc[:, h * HEAD_DIM:(h + 1) * HEAD_DIM]
            vm_ref[rows_h, :] = acc[:, half + h * HEAD_DIM:half + (h + 1) * HEAD_DIM]
        if attn_refs:
            kbm_ref, vtm_ref, kmm_ref = attn_refs[:3]
            for h in range(MOBA_KV_HEADS):
                k_h = acc[:, h * HEAD_DIM:(h + 1) * HEAD_DIM]
                kbm_ref[h] = k_h.astype(BF16)
                kmm_ref[0, h] = jnp.sum(k_h.reshape(tm // blk, blk, HEAD_DIM), axis=1) * (1.0 / blk)
                for r in range(tm // blk):
                    v_hr = acc[r * blk:(r + 1) * blk, half + h * HEAD_DIM:half + (h + 1) * HEAD_DIM]
                    vtm_ref[h, r] = v_hr.T.astype(BF16)

    @pl.when(n == 2)
    def _():
        zm_ref[...] = _silu(tile()).astype(BF16)

    @pl.when(n == 3)
    def _():
        qd_ref[...] = tile().astype(BF16)

    @pl.when(n == 4)
    def _():
        acc = tile()
        for hk in range(DIFF_KV_HEADS):
            for c in range(2):
                rows_c = pl.ds(c * DIFF_KV_HEADS + hk, tm, stride=2 * DIFF_KV_HEADS)
                lo = hk * DIFF_V_DIM + c * HEAD_DIM
                kd_ref[rows_c, :] = acc[:, lo:lo + HEAD_DIM]
                vd_ref[rows_c, :] = acc[:, half + lo:half + lo + HEAD_DIM]
        if attn_refs:
            kbd_ref, vtd_ref = attn_refs[3:]
            for hk in range(DIFF_KV_HEADS):
                kbd_ref[hk] = acc[:, hk * DIFF_V_DIM:(hk + 1) * DIFF_V_DIM].astype(BF16)
                for r in range(tm // blk):
                    v_hr = acc[r * blk:(r + 1) * blk, half + hk * DIFF_V_DIM:half + (hk + 1) * DIFF_V_DIM]
                    vtd_ref[hk, r] = v_hr.T.astype(BF16)

    @pl.when(n == 5)
    def _():
        zd_ref[...] = _silu(tile()).astype(BF16)

    @pl.when(n >= 6)
    def _():
        gt_ref[...] = _sigmoid(tile()).astype(BF16)


def _proj(x2d, w, tm, attn_layouts):
    m_rows = x2d.shape[0]
    n_tiles = w.shape[1] // PROJ_TN
    m_last = m_rows // tm - 1

    def out_map(lo, hi):
        return lambda n, m: (jnp.where(n < lo, 0, jnp.where(n >= hi, m_last, m)), jnp.clip(n - lo, 0, hi - lo - 1))

    def row_tile(lo):
        return lambda n, m: jnp.where(n < lo, 0, jnp.where(n > lo, m_last, m))

    out_shape = (
        jax.ShapeDtypeStruct((m_rows, 1024), BF16),
        jax.ShapeDtypeStruct((m_rows * 4, HEAD_DIM), F32),
        jax.ShapeDtypeStruct((m_rows * 4, HEAD_DIM), F32),
        jax.ShapeDtypeStruct((m_rows, 1024), BF16),
        jax.ShapeDtypeStruct((m_rows, 1024), BF16),
        jax.ShapeDtypeStruct((m_rows * 4, HEAD_DIM), F32),
        jax.ShapeDtypeStruct((m_rows * 4, HEAD_DIM), F32),
        jax.ShapeDtypeStruct((m_rows, 1024), BF16),
        jax.ShapeDtypeStruct((m_rows, 4096), BF16),
    )
    out_specs = (
        pl.BlockSpec((tm, PROJ_TN), out_map(0, 1)),
        pl.BlockSpec((tm * 4, HEAD_DIM), out_map(1, 2)),
        pl.BlockSpec((tm * 4, HEAD_DIM), out_map(1, 2)),
        pl.BlockSpec((tm, PROJ_TN), out_map(2, 3)),
        pl.BlockSpec((tm, PROJ_TN), out_map(3, 4)),
        pl.BlockSpec((tm * 4, HEAD_DIM), out_map(4, 5)),
        pl.BlockSpec((tm * 4, HEAD_DIM), out_map(4, 5)),
        pl.BlockSpec((tm, PROJ_TN), out_map(5, 6)),
        pl.BlockSpec((tm, PROJ_TN), out_map(6, 10)),
    )
    if attn_layouts:
        blk = MOBA_BLOCK
        rt = tm // blk
        tile_m, tile_d = row_tile(1), row_tile(4)
        out_shape += (
            jax.ShapeDtypeStruct((MOBA_KV_HEADS, m_rows, HEAD_DIM), BF16),
            jax.ShapeDtypeStruct((MOBA_KV_HEADS, m_rows // blk, HEAD_DIM, blk), BF16),
            jax.ShapeDtypeStruct((m_rows // tm, MOBA_KV_HEADS, rt, HEAD_DIM), F32),
            jax.ShapeDtypeStruct((DIFF_KV_HEADS, m_rows, DIFF_V_DIM), BF16),
            jax.ShapeDtypeStruct((DIFF_KV_HEADS, m_rows // blk, DIFF_V_DIM, blk), BF16),
        )
        out_specs += (
            pl.BlockSpec((MOBA_KV_HEADS, tm, HEAD_DIM), lambda n, m: (0, tile_m(n, m), 0)),
            pl.BlockSpec((MOBA_KV_HEADS, rt, HEAD_DIM, blk), lambda n, m: (0, tile_m(n, m), 0, 0)),
            pl.BlockSpec((1, MOBA_KV_HEADS, rt, HEAD_DIM), lambda n, m: (tile_m(n, m), 0, 0, 0)),
            pl.BlockSpec((DIFF_KV_HEADS, tm, DIFF_V_DIM), lambda n, m: (0, tile_d(n, m), 0)),
            pl.BlockSpec((DIFF_KV_HEADS, rt, DIFF_V_DIM, blk), lambda n, m: (0, tile_d(n, m), 0, 0)),
        )
    return pl.pallas_call(
        _proj_kernel,
        grid=(n_tiles, m_rows // tm),
        in_specs=[pl.BlockSpec((tm, D_MODEL), lambda n, m: (m, 0)),
                  pl.BlockSpec((D_MODEL, PROJ_TN), lambda n, m: (0, n))],
        out_specs=out_specs,
        out_shape=out_shape,
        scratch_shapes=[pltpu.VMEM((D_MODEL, PROJ_TN), BF16)],
        compiler_params=pltpu.CompilerParams(
            dimension_semantics=("arbitrary", "arbitrary"), vmem_limit_bytes=VMEM_LIMIT_V7X),
        name="proj",
    )(x2d, w)


LOG2E = 1.4426950408889634
SCALE_L2 = SCALE * LOG2E
MASKED = 1e30


def _alibi_tiles_t(blk, qlen, q_off):
    c_i = lax.broadcasted_iota(jnp.int32, (blk, 2 * qlen), 0)
    q_i = lax.broadcasted_iota(jnp.int32, (blk, 2 * qlen), 1)
    g = q_i // qlen
    rel = (q_off + q_i - g * qlen - c_i).astype(F32)
    return rel, g[:1], rel >= 0.0


def _slope_row_l2(g_row, head0, slope_mul):
    return _pow2_neg(slope_mul * (head0 + g_row + 1)) * LOG2E


def _lambda_full(lq1_ref, lk1_ref, lq2_ref, lk2_ref):
    a = jnp.sum(lq1_ref[...] * lk1_ref[...], axis=1, keepdims=True)
    b = jnp.sum(lq2_ref[...] * lk2_ref[...], axis=1, keepdims=True)
    return jnp.exp(a) - jnp.exp(b) + LAM0


def _moba_prompt_steps(j, q_ref, kb_ref, vt_ref, km_ref, z_ref, o_ref, bias_sc, acc_sc, st_sc, geom):
    blk = MOBA_BLOCK
    n_kv = MOBA_KV_HEADS
    rel, g_row, causal = geom
    qlen = rel.shape[1] // 2
    env = {}

    def start():
        qss, ranks = [], []
        for hk in range(n_kv):
            qj = q_ref[0, :, hk * 2 * HEAD_DIM:(hk + 1) * 2 * HEAD_DIM]
            qs = jnp.concatenate([qj[:, :HEAD_DIM], qj[:, HEAD_DIM:]], axis=0)
            kmean = jnp.concatenate([km_ref[t, hk] for t in range(km_ref.shape[0])], axis=0)
            kmean_hi = kmean.astype(BF16)
            kmean_lo = (kmean - kmean_hi.astype(F32)).astype(BF16)
            gate = (lax.dot_general(kmean_hi, qs, _NT, preferred_element_type=F32)
                    + lax.dot_general(kmean_lo, qs, _NT, preferred_element_type=F32))
            sub = lax.broadcasted_iota(jnp.int32, gate.shape, 0)
            gate = jnp.where(sub < j, gate, NEG_INF)
            rank = jnp.zeros(gate.shape, F32)
            for m in range(gate.shape[0]):
                gm = gate[m:m + 1, :]
                ge = jnp.where(gm >= gate, 1.0, 0.0)
                gt = jnp.where(gm > gate, 1.0, 0.0)
                rank = rank + jnp.where(sub > m, ge, gt)
            k_own = kb_ref[hk, pl.ds(pl.multiple_of(j * blk, blk), blk), :]
            u = lax.dot_general(k_own, qs, _NT, preferred_element_type=F32) * SCALE_L2 - bias_sc[hk]
            u = jnp.where(causal, u, -MASKED)
            m0 = jnp.max(u, axis=0, keepdims=True)
            p = jnp.exp2(u - m0)
            st_sc[2 * hk:2 * hk + 1, :] = m0
            st_sc[2 * hk + 1:2 * hk + 2, :] = jnp.sum(p, axis=0, keepdims=True)
            acc_sc[hk] = jnp.dot(vt_ref[hk, j], p.astype(BF16), preferred_element_type=F32)
            qss.append(qs)
            ranks.append(rank)
        env["qss"], env["ranks"] = qss, ranks

    def piece(n):
        dist_blocks = (j - n).astype(F32)
        for hk in range(n_kv):
            m_run = st_sc[2 * hk:2 * hk + 1, :]
            l_run = st_sc[2 * hk + 1:2 * hk + 2, :]
            k_n = kb_ref[hk, pl.ds(pl.multiple_of(n * blk, blk), blk), :]
            u = lax.dot_general(k_n, env["qss"][hk], _NT, preferred_element_type=F32) * SCALE_L2 - bias_sc[hk]
            rank = env["ranks"][hk]
            sub = lax.broadcasted_iota(jnp.int32, rank.shape, 0)
            rk = jnp.sum(jnp.where(sub == n, rank, 0.0), axis=0, keepdims=True)
            slope_blk = _slope_row_l2(g_row, 2 * hk, 1) * float(blk)
            rowc = jnp.where(rk < float(MOBA_TOPK), slope_blk * dist_blocks, MASKED)
            m_new = jnp.maximum(m_run, jnp.max(u, axis=0, keepdims=True) - rowc)
            alpha = jnp.exp2(m_run - m_new)
            p = jnp.exp2(u - (rowc + m_new))
            st_sc[2 * hk:2 * hk + 1, :] = m_new
            st_sc[2 * hk + 1:2 * hk + 2, :] = alpha * l_run + jnp.sum(p, axis=0, keepdims=True)
            acc_sc[hk] = alpha * acc_sc[hk] + jnp.dot(vt_ref[hk, n], p.astype(BF16), preferred_element_type=F32)

    def finish():
        for hk in range(n_kv):
            ot = acc_sc[hk] / st_sc[2 * hk + 1:2 * hk + 2, :]
            o2 = jnp.concatenate([ot[:, :qlen].T, ot[:, qlen:].T], axis=1)
            cols = slice(hk * 2 * HEAD_DIM, (hk + 1) * 2 * HEAD_DIM)
            o_ref[0, :, cols] = (o2 * z_ref[0, :, cols].astype(F32)).astype(BF16)

    return start, piece, finish


def _diff_prompt_steps(j, lam, q_ref, kb_ref, vt_ref, z_ref, gn_ref, o_ref, bias_sc, acc_sc, st_sc, geom):
    blk = MOBA_BLOCK
    n_kv = DIFF_KV_HEADS
    rel, g_row, causal = geom
    qlen = rel.shape[1] // 2
    chains = [(hk, mi) for hk in range(n_kv) for mi in range(2)]
    env = {}

    def scores(hk, mi, n):
        k_n = kb_ref[hk, pl.ds(pl.multiple_of(n * blk, blk), blk), mi * HEAD_DIM:(mi + 1) * HEAD_DIM]
        return lax.dot_general(k_n, env["qss"][hk, mi], _NT, preferred_element_type=F32) * SCALE_L2 - bias_sc[hk]

    def start():
        qss = {}
        for hk in range(n_kv):
            qj = q_ref[0, :, hk * 4 * HEAD_DIM:(hk + 1) * 4 * HEAD_DIM]
            for mi in range(2):
                qss[hk, mi] = jnp.concatenate([qj[:, mi * HEAD_DIM:(mi + 1) * HEAD_DIM],
                                               qj[:, (2 + mi) * HEAD_DIM:(3 + mi) * HEAD_DIM]], axis=0)
        env["qss"] = qss
        for ci, (hk, mi) in enumerate(chains):
            u = jnp.where(causal, scores(hk, mi, j), -MASKED)
            m0 = jnp.max(u, axis=0, keepdims=True)
            p = jnp.exp2(u - m0)
            st_sc[2 * ci:2 * ci + 1, :] = m0
            st_sc[2 * ci + 1:2 * ci + 2, :] = jnp.sum(p, axis=0, keepdims=True)
            acc_sc[ci] = jnp.dot(vt_ref[hk, j], p.astype(BF16), preferred_element_type=F32)

    def piece(n):
        dist_blocks = (j - n).astype(F32)
        for ci, (hk, mi) in enumerate(chains):
            m_run = st_sc[2 * ci:2 * ci + 1, :]
            l_run = st_sc[2 * ci + 1:2 * ci + 2, :]
            rowc = _slope_row_l2(g_row, 2 * hk, 2) * float(blk) * dist_blocks
            u = scores(hk, mi, n)
            m_new = jnp.maximum(m_run, jnp.max(u, axis=0, keepdims=True) - rowc)
            alpha = jnp.exp2(m_run - m_new)
            p = jnp.exp2(u - (rowc + m_new))
            st_sc[2 * ci:2 * ci + 1, :] = m_new
            st_sc[2 * ci + 1:2 * ci + 2, :] = alpha * l_run + jnp.sum(p, axis=0, keepdims=True)
            acc_sc[ci] = alpha * acc_sc[ci] + jnp.dot(vt_ref[hk, n], p.astype(BF16), preferred_element_type=F32)

    def finish():
        for hk in range(n_kv):
            c1, c2 = 2 * hk, 2 * hk + 1
            ot = (acc_sc[c1] / st_sc[2 * c1 + 1:2 * c1 + 2, :]
                  - lam * (acc_sc[c2] / st_sc[2 * c2 + 1:2 * c2 + 2, :]))
            ms = jnp.mean(ot * ot, axis=0, keepdims=True)
            ot = ot * lax.rsqrt(ms + RMS_EPS)
            gn = gn_ref[hk]
            o2 = jnp.concatenate([ot[:, :qlen].T * gn[0:1, :], ot[:, qlen:].T * gn[1:2, :]], axis=1) * (1.0 - LAM0)
            cols = slice(hk * 2 * DIFF_V_DIM, (hk + 1) * 2 * DIFF_V_DIM)
            o_ref[0, :, cols] = (o2 * z_ref[0, :, cols].astype(F32)).astype(BF16)

    return start, piece, finish


ROWS_PG = PAGE_SIZE * 4
DEC1_PAGES_PER_CHUNK = 8
PAGES_PER_BLOCK = MOBA_BLOCK // PAGE_SIZE


def _pair_lo(x, even):
    return jnp.where(even, x, pltpu.roll(x, 1, axis=0))


def _pair_hi(x, even):
    return jnp.where(even, pltpu.roll(x, x.shape[0] - 1, axis=0), x)


def _dec1_attn_kernel(pt_ref, qd_ref, qm_ref, kdn_ref, vdn_ref, zd_ref, gn_ref, lq1_ref, lk1_ref, lq2_ref, lk2_ref,
                      ck_hbm, cv_hbm, cmk_hbm,
                      kbm_ref, vtm_ref, kmm_ref, kbd_ref, vtd_ref, gnp_ref,
                      qm0_ref, zm0_ref, qd0_ref, zd0_ref, qm1_ref, zm1_ref, qd1_ref, zd1_ref,
                      ad_ref, sel_ref, am0_ref, ad0_ref, am1_ref, ad1_ref,
                      kbuf, vbuf, mbuf, sems, ks_sc, bias_m, bias_d, acc_m, acc_d, st_m, st_d,
                      *, n_pages, past_len, n_blk):
    b = pl.program_id(0)
    n_b = pl.num_programs(0)
    j = b % n_blk
    j_sub = (j, n_blk - 1 - j)
    qlen = MOBA_BLOCK // 2
    ppc = DEC1_PAGES_PER_CHUNK
    n_chunks = n_pages // ppc
    rows = ppc * ROWS_PG
    streams = ((ck_hbm, kbuf), (cv_hbm, vbuf), (cmk_hbm, mbuf))

    def chunk_copies(bb, c, slot):
        out = []
        for k in range(ppc):
            page = pt_ref[bb, c * ppc + k]
            for si, (src, dst) in enumerate(streams):
                out.append(pltpu.make_async_copy(
                    src.at[page], dst.at[slot, pl.ds(k * ROWS_PG, ROWS_PG), :], sems.at[si, slot]))
        return out

    geoms = [_alibi_tiles_t(MOBA_BLOCK, qlen, t * qlen) for t in range(2)]

    @pl.when(b == 0)
    def _():
        for cp in chunk_copies(0, 0, 0) + chunk_copies(0, 1, 1):
            cp.start()
        for t in range(2):
            for hk in range(MOBA_KV_HEADS):
                bias_m[t, hk] = geoms[t][0] * _slope_row_l2(geoms[t][1], 2 * hk, 1)
            for hk in range(DIFF_KV_HEADS):
                bias_d[t, hk] = geoms[t][0] * _slope_row_l2(geoms[t][1], 2 * hk, 2)

    lam = _lambda_full(lq1_ref, lk1_ref, lq2_ref, lk2_ref)
    moba_io = ((qm0_ref, zm0_ref, am0_ref), (qm1_ref, zm1_ref, am1_ref))
    diff_io = ((qd0_ref, zd0_ref, ad0_ref), (qd1_ref, zd1_ref, ad1_ref))
    moba, diff = [], []
    for t in range(2):
        q_r, z_r, o_r = moba_io[t]
        moba.append(_moba_prompt_steps(j_sub[t], q_r, kbm_ref, vtm_ref, kmm_ref, z_r, o_r,
                                       bias_m.at[t], acc_m.at[t], st_m.at[t], geoms[t]))
        q_r, z_r, o_r = diff_io[t]
        diff.append(_diff_prompt_steps(j_sub[t], lam, q_r, kbd_ref, vtd_ref, z_r, gnp_ref, o_r,
                                       bias_d.at[t], acc_d.at[t], st_d.at[t], geoms[t]))
    for t in range(2):
        moba[t][0]()
        diff[t][0]()

    q8 = qd_ref[0]
    row = lax.broadcasted_iota(jnp.int32, (8, rows), 0)
    col = lax.broadcasted_iota(jnp.int32, (8, rows), 1)
    row1 = lax.broadcasted_iota(jnp.int32, (8, HEAD_DIM), 0)
    even = (row & 1) == 0
    even1 = (row1 & 1) == 0
    valid = (col & 3) == ((row & 1) * 2 + (row >> 2))
    slope = _pow2_neg(2 * ((row >> 1) + 1))
    posf = (col >> 2).astype(F32)

    def chunk_compute(c, slot, carry):
        m_old, l_old, acc = carry
        s = lax.dot_general(q8, kbuf[slot].astype(BF16), _NT, preferred_element_type=F32) * SCALE
        base = jnp.asarray(past_len - c * (ppc * PAGE_SIZE)).astype(F32)
        s = jnp.where(valid, s - slope * (base - posf), NEG_INF)
        m_new = jnp.maximum(m_old, jnp.max(s, axis=1, keepdims=True))
        alpha = jnp.exp(m_old - m_new)
        p = jnp.where(valid, jnp.exp(s - m_new[:, :1]), 0.0)
        l_new = alpha * l_old + jnp.sum(p, axis=1, keepdims=True)
        p_r = pltpu.roll(p, 2, axis=1)
        p_l = pltpu.roll(p, rows - 2, axis=1)
        w1 = jnp.where(even, p, pltpu.roll(p_r, 1, axis=0))
        w2 = jnp.where(even, pltpu.roll(p_l, 7, axis=0), p)
        w = jnp.concatenate([w1, w2], axis=0).astype(BF16)
        a16 = jnp.concatenate([_pair_lo(alpha, even1), _pair_hi(alpha, even1)], axis=0)
        acc = a16 * acc + jnp.dot(w, vbuf[slot].astype(BF16), preferred_element_type=F32)
        bpc = ppc // PAGES_PER_BLOCK
        ksum = jnp.sum(mbuf[slot].reshape(bpc, PAGES_PER_BLOCK * ROWS_PG // 8, 8, HEAD_DIM), axis=1)
        ks_sc[pl.ds(c * bpc, bpc)] = ksum
        return m_new, l_new, acc

    def pair_body(c2, carry):
        for u in range(2):
            c = 2 * c2 + u
            for cp in chunk_copies(b, c, u):
                cp.wait()
            carry = chunk_compute(c, u, carry)
            in_b = c + 2 < n_chunks

            @pl.when(in_b | (b + 1 < n_b))
            def _():
                for cp in chunk_copies(jnp.where(in_b, b, b + 1), jnp.where(in_b, c + 2, c + 2 - n_chunks), u):
                    cp.start()

            for t in range(2):
                pl.when(c2 < j_sub[t])(functools.partial((moba if u == 0 else diff)[t][1], c2))
        return carry

    init = (jnp.full((8, HEAD_DIM), NEG_INF, F32), jnp.zeros((8, HEAD_DIM), F32), jnp.zeros((16, HEAD_DIM), F32))
    m_o, l_o, acc = lax.fori_loop(0, n_chunks // 2, pair_body, init)
    for t in range(2):
        moba[t][2]()
        diff[t][2]()

    qf = q8.astype(F32)
    s_new = jnp.sum(qf * kdn_ref[0], axis=1, keepdims=True) * SCALE
    m_f = jnp.maximum(m_o, s_new)
    a_f = jnp.exp(m_o - m_f)
    p_new = jnp.exp(s_new - m_f)
    l_f = a_f * l_o + p_new
    vdn = vdn_ref[0]
    acc1 = _pair_lo(a_f, even1) * acc[:8] + _pair_lo(p_new, even1) * vdn
    acc2 = _pair_hi(a_f, even1) * acc[8:] + _pair_hi(p_new, even1) * vdn
    o = acc1 / _pair_lo(l_f, even1) - lam * (acc2 / _pair_hi(l_f, even1))
    ss = jnp.broadcast_to(jnp.sum(o * o, axis=1, keepdims=True), (8, HEAD_DIM))
    ss = ss + jnp.where(even1, pltpu.roll(ss, 7, axis=0), pltpu.roll(ss, 1, axis=0))
    o = o * lax.rsqrt(ss * (1.0 / DIFF_V_DIM) + RMS_EPS) * gn_ref[...] * (1.0 - LAM0)
    ad_ref[0] = (o * zd_ref[0].astype(F32)).astype(BF16)

    n_cols = ks_sc.shape[0] * 8
    ks = ks_sc[...].reshape(n_cols, HEAD_DIM)
    ks_hi = ks.astype(BF16)
    ks_lo = (ks - ks_hi.astype(F32)).astype(BF16)
    qm8 = qm_ref[0]
    g_all = (lax.dot_general(qm8, ks_hi, _NT, preferred_element_type=F32)
             + lax.dot_general(qm8, ks_lo, _NT, preferred_element_type=F32)) * (1.0 / MOBA_BLOCK)
    g2 = g_all + pltpu.roll(g_all, n_cols - 4, axis=1)
    rw = lax.broadcasted_iota(jnp.int32, (8, n_cols), 0)
    cl = lax.broadcasted_iota(jnp.int32, (8, n_cols), 1)
    clf = cl.astype(F32)
    gm = jnp.where((cl & 7) == (rw >> 1), g2, NEG_INF)
    lane = lax.broadcasted_iota(jnp.int32, (8, HEAD_DIM), 1)
    sel = jnp.zeros((8, HEAD_DIM), jnp.int32)
    for t in range(MOBA_TOPK):
        mx = jnp.max(gm, axis=1, keepdims=True)
        idx = jnp.min(jnp.where(gm == mx, clf, float(n_cols)), axis=1, keepdims=True)
        gm = jnp.where(clf == idx, NEG_INF, gm)
        sel = jnp.where(lane == t, idx.astype(jnp.int32) >> 3, sel)
    sel_ref[0] = sel


def _dec1_attn(page_table, qd8, qm8, kdn8, vdn8, zd8, gn8, lams, ck3, cv3, cmk3, prompt_ops, past_len, batch, seq):
    dec_batch, n_pages = page_table.shape
    blk = MOBA_BLOCK
    n_blk = seq // blk
    n_chunks = n_pages // DEC1_PAGES_PER_CHUNK
    assert dec_batch == batch * n_blk and n_chunks == 2 * n_blk and DEC1_PAGES_PER_CHUNK % PAGES_PER_BLOCK == 0
    n_blocks = n_pages // PAGES_PER_BLOCK
    rows = DEC1_PAGES_PER_CHUNK * ROWS_PG
    rt = PROJ_TM // blk
    row8 = lambda: pl.BlockSpec((1, 8, HEAD_DIM), lambda b, pt: (b, 0, 0))
    vec = pl.BlockSpec((1, HEAD_DIM), lambda b, pt: (0, 0))
    hbm = lambda: pl.BlockSpec(memory_space=pl.ANY)
    qlen = blk // 2
    half_in = (lambda: pl.BlockSpec((1, qlen, 1024), lambda b, pt: (b // n_blk, 2 * (b % n_blk), 0)),
               lambda: pl.BlockSpec((1, qlen, 1024), lambda b, pt: (b // n_blk, 2 * (n_blk - 1 - b % n_blk) + 1, 0)))
    half_out = (lambda: pl.BlockSpec((1, qlen, 1024), lambda b, pt: (b // n_blk, b % n_blk, 0)),
                lambda: pl.BlockSpec((1, qlen, 1024), lambda b, pt: (b // n_blk, n_blk - 1 - b % n_blk, 0)))
    grid_spec = pltpu.PrefetchScalarGridSpec(
        num_scalar_prefetch=1,
        grid=(dec_batch,),
        in_specs=[row8(), row8(), row8(), row8(), row8(),
                  pl.BlockSpec((8, HEAD_DIM), lambda b, pt: (0, 0)),
                  vec, vec, vec, vec,
                  hbm(), hbm(), hbm(),
                  pl.BlockSpec((MOBA_KV_HEADS, seq, HEAD_DIM), lambda b, pt: (0, b // n_blk, 0)),
                  pl.BlockSpec((MOBA_KV_HEADS, n_blk, HEAD_DIM, blk), lambda b, pt: (0, b // n_blk, 0, 0)),
                  pl.BlockSpec((n_blk // rt, MOBA_KV_HEADS, rt, HEAD_DIM), lambda b, pt: (b // n_blk, 0, 0, 0)),
                  pl.BlockSpec((DIFF_KV_HEADS, seq, DIFF_V_DIM), lambda b, pt: (0, b // n_blk, 0)),
                  pl.BlockSpec((DIFF_KV_HEADS, n_blk, DIFF_V_DIM, blk), lambda b, pt: (0, b // n_blk, 0, 0)),
                  pl.BlockSpec((DIFF_KV_HEADS, 2, DIFF_V_DIM), lambda b, pt: (0, 0, 0)),
                  half_in[0](), half_in[0](), half_in[0](), half_in[0](),
                  half_in[1](), half_in[1](), half_in[1](), half_in[1]()],
        out_specs=(pl.BlockSpec((1, 8, HEAD_DIM), lambda b, pt: (b, 0, 0)),
                   pl.BlockSpec((1, 8, HEAD_DIM), lambda b, pt: (b, 0, 0)),
                   half_out[0](), half_out[0](), half_out[1](), half_out[1]()),
        scratch_shapes=[pltpu.VMEM((2, rows, HEAD_DIM), F32), pltpu.VMEM((2, rows, HEAD_DIM), F32),
                        pltpu.VMEM((2, rows, HEAD_DIM), F32), pltpu.SemaphoreType.DMA((3, 2)),
                        pltpu.VMEM((n_blocks, 8, HEAD_DIM), F32),
                        pltpu.VMEM((2, MOBA_KV_HEADS, blk, 2 * qlen), F32),
                        pltpu.VMEM((2, DIFF_KV_HEADS, blk, 2 * qlen), F32),
                        pltpu.VMEM((2, MOBA_KV_HEADS, HEAD_DIM, 2 * qlen), F32),
                        pltpu.VMEM((2, 2 * DIFF_KV_HEADS, DIFF_V_DIM, 2 * qlen), F32),
                        pltpu.VMEM((2, 2 * MOBA_KV_HEADS, 2 * qlen), F32),
                        pltpu.VMEM((2, 4 * DIFF_KV_HEADS, 2 * qlen), F32)],
    )
    return pl.pallas_call(
        functools.partial(_dec1_attn_kernel, n_pages=n_pages, past_len=past_len, n_blk=n_blk),
        grid_spec=grid_spec,
        out_shape=(jax.ShapeDtypeStruct((dec_batch, 8, HEAD_DIM), BF16),
                   jax.ShapeDtypeStruct((dec_batch, 8, HEAD_DIM), jnp.int32),
                   ) + (jax.ShapeDtypeStruct((batch, seq // 2, 1024), BF16),) * 4,
        compiler_params=pltpu.CompilerParams(
            dimension_semantics=("arbitrary",), vmem_limit_bytes=VMEM_LIMIT_V7X),
        name="dec1_attn",
    )(page_table, qd8, qm8, kdn8, vdn8, zd8, gn8, *lams, ck3, cv3, cmk3, *prompt_ops)


def _dec2_kernel(pt_ref, sel_ref, qm_ref, kmn_ref, vmn_ref, zm_ref, ck_hbm, cv_hbm, am_ref, kbuf, vbuf, sems, o_sc,
                 *, past_len):
    g = pl.program_id(0)
    n_g = pl.num_programs(0)
    hk = g % MOBA_KV_HEADS
    slot = g % 2
    blk_rows = PAGES_PER_BLOCK * ROWS_PG
    head_rows = MOBA_TOPK * blk_rows
    rows = 2 * head_rows

    def step_copies(gg, sl):
        bb = gg // MOBA_KV_HEADS
        kk = gg % MOBA_KV_HEADS
        out = []
        for gi in range(2):
            for t in range(MOBA_TOPK):
                blk = sel_ref[bb, (2 * kk + gi) * MOBA_TOPK + t]
                for f in range(PAGES_PER_BLOCK):
                    page = pt_ref[bb, blk * PAGES_PER_BLOCK + f]
                    dst = pl.ds(gi * head_rows + (t * PAGES_PER_BLOCK + f) * ROWS_PG, ROWS_PG)
                    out.append(pltpu.make_async_copy(ck_hbm.at[page], kbuf.at[sl, dst, :], sems.at[0, sl]))
                    out.append(pltpu.make_async_copy(cv_hbm.at[page], vbuf.at[sl, dst, :], sems.at[1, sl]))
        return out

    @pl.when(g == 0)
    def _():
        for cp in step_copies(0, 0):
            cp.start()

    @pl.when(g + 1 < n_g)
    def _():
        for cp in step_copies(g + 1, 1 - slot):
            cp.start()

    for cp in step_copies(g, slot):
        cp.wait()

    q8 = qm_ref[0]
    row = lax.broadcasted_iota(jnp.int32, (8, rows), 0)
    col = lax.broadcasted_iota(jnp.int32, (8, rows), 1)
    row1 = lax.broadcasted_iota(jnp.int32, (8, HEAD_DIM), 0)
    b = g // MOBA_KV_HEADS
    blk_col = jnp.zeros((8, rows), jnp.int32)
    for gi in range(2):
        for t in range(MOBA_TOPK):
            lo = gi * head_rows + t * blk_rows
            seg = (col >= lo) & (col < lo + blk_rows)
            blk_col = jnp.where(seg, sel_ref[b, (2 * hk + gi) * MOBA_TOPK + t], blk_col)
    kpos = blk_col * MOBA_BLOCK + ((col & (blk_rows - 1)) >> 2)
    s = lax.dot_general(q8, kbuf[slot].astype(BF16), _NT, preferred_element_type=F32) * SCALE
    s = s - _pow2_neg(row + 1) * (past_len - kpos).astype(F32)
    head_col = 2 * hk + jnp.where(col >= head_rows, 1, 0)
    valid = (row == head_col) & ((col & 3) == hk)
    s = jnp.where(valid, s, NEG_INF)
    s_new = jnp.sum(q8.astype(F32) * kmn_ref[0], axis=1, keepdims=True) * SCALE
    m = jnp.maximum(jnp.max(s, axis=1, keepdims=True), s_new)
    p = jnp.where(valid, jnp.exp(s - m), 0.0)
    p_new = jnp.exp(s_new - m)
    l = jnp.sum(p, axis=1, keepdims=True) + p_new
    o = (jnp.dot(p.astype(BF16), vbuf[slot].astype(BF16), preferred_element_type=F32) + p_new * vmn_ref[0]) / l

    @pl.when(hk == 0)
    def _():
        o_sc[...] = jnp.zeros(o_sc.shape, F32)

    o_sc[...] = jnp.where((row1 >> 1) == hk, o, o_sc[...])

    @pl.when(hk == MOBA_KV_HEADS - 1)
    def _():
        am_ref[0] = (o_sc[...] * zm_ref[0].astype(F32)).astype(BF16)


def _dec2(page_table, sel, qm8, kmn8, vmn8, zm8, cmk3, cmv3, past_len):
    dec_batch = page_table.shape[0]
    rows = 2 * MOBA_TOPK * PAGES_PER_BLOCK * ROWS_PG
    row8 = lambda: pl.BlockSpec((1, 8, HEAD_DIM), lambda g, pt, sl: (g // MOBA_KV_HEADS, 0, 0))
    hbm = lambda: pl.BlockSpec(memory_space=pl.ANY)
    grid_spec = pltpu.PrefetchScalarGridSpec(
        num_scalar_prefetch=2,
        grid=(dec_batch * MOBA_KV_HEADS,),
        in_specs=[row8(), row8(), row8(), row8(), hbm(), hbm()],
        out_specs=pl.BlockSpec((1, 8, HEAD_DIM), lambda g, pt, sl: (g // MOBA_KV_HEADS, 0, 0)),
        scratch_shapes=[pltpu.VMEM((2, rows, HEAD_DIM), F32), pltpu.VMEM((2, rows, HEAD_DIM), F32),
                        pltpu.SemaphoreType.DMA((2, 2)), pltpu.VMEM((8, HEAD_DIM), F32)],
    )
    return pl.pallas_call(
        functools.partial(_dec2_kernel, past_len=past_len),
        grid_spec=grid_spec,
        out_shape=jax.ShapeDtypeStruct((dec_batch, 8, HEAD_DIM), BF16),
        compiler_params=pltpu.CompilerParams(
            dimension_semantics=("arbitrary",), vmem_limit_bytes=VMEM_LIMIT_V7X),
        name="dec2",
    )(page_table, sel, qm8, kmn8, vmn8, zm8, cmk3, cmv3)


def _post_kernel(*refs):
    n_act = len(refs) - 8
    g_ref, x_ref, wmo_ref, wdo_ref, wo_ref, lng_ref, lnb_ref, y_ref = refs[n_act:]
    if n_act == 2:
        am, ad = refs[0][...], refs[1][...]
    else:
        am = jnp.concatenate([refs[0][0], refs[1][0]], axis=0)
        ad = jnp.concatenate([refs[2][0], refs[3][0]], axis=0)
    ym = jnp.dot(am, wmo_ref[...], preferred_element_type=F32)
    yd = jnp.dot(ad, wdo_ref[...], preferred_element_type=F32)
    g = g_ref[...].astype(F32)
    merged = g[:, :D_MODEL] * ym + g[:, D_MODEL:] * yd
    f = jnp.dot(merged.astype(BF16), wo_ref[...], preferred_element_type=F32)
    u = ALPHA * x_ref[...] + f
    mu = jnp.mean(u, axis=1, keepdims=True)
    d = u - mu
    var = jnp.mean(d * d, axis=1, keepdims=True)
    y_ref[...] = d * lax.rsqrt(var + LN_EPS) * lng_ref[...] + lnb_ref[...]


def _post(acts, gt, x2d, wmo, wdo, wo, lng, lnb, tm):
    m_rows = x2d.shape[0]
    rowblk = lambda w: pl.BlockSpec((tm, w), lambda m: (m, 0))
    full = lambda a: pl.BlockSpec(a.shape, lambda m: (0, 0))
    if len(acts) == 2:
        act_specs = [rowblk(1024), rowblk(1024)]
    else:
        assert tm == MOBA_BLOCK
        per_batch = acts[0].shape[1] // (tm // 2)
        act_specs = [pl.BlockSpec((1, tm // 2, 1024), lambda m: (m // per_batch, m % per_batch, 0))] * 4
    return pl.pallas_call(
        _post_kernel,
        grid=(m_rows // tm,),
        in_specs=act_specs + [rowblk(N_BRANCH * D_MODEL), rowblk(D_MODEL),
                              full(wmo), full(wdo), full(wo), full(lng), full(lnb)],
        out_specs=rowblk(D_MODEL),
        out_shape=jax.ShapeDtypeStruct((m_rows, D_MODEL), F32),
        compiler_params=pltpu.CompilerParams(
            dimension_semantics=("parallel",), vmem_limit_bytes=VMEM_LIMIT_V7X),
        name="post",
    )(*acts, gt, x2d, wmo, wdo, wo, lng, lnb)


def kernel(x_prompt, x_sample, cache_moba_k, cache_moba_v, cache_diff_k, cache_diff_v, page_table, w_in, w_moba_o, w_diff_o, w_out, lambda_q1, lambda_k1, lambda_q2, lambda_k2, diff_norm_g, ln_g, ln_b):
    assert w_in.shape[0] == DEPTH == 1
    batch, seq, _ = x_prompt.shape
    dec_batch, dec_seq, _ = x_sample.shape
    assert dec_seq == 1 and seq % MOBA_BLOCK == 0
    n_pages = page_table.shape[1]
    past_len = n_pages * PAGE_SIZE
    assert past_len % MOBA_BLOCK == 0 and past_len // MOBA_BLOCK >= MOBA_TOPK
    n_phys = cache_moba_k.shape[1]

    wmo = w_moba_o[0].astype(BF16)
    wdo = w_diff_o[0].astype(BF16)
    wo = w_out[0].astype(BF16)
    lams = (lambda_q1, lambda_k1, lambda_q2, lambda_k2)
    lng = ln_g.reshape(1, D_MODEL)
    lnb = ln_b.reshape(1, D_MODEL)

    xp = x_prompt.reshape(batch * seq, D_MODEL)
    qm, km, vm, zm, qd, kd, vd, zd, gt, kbm, vtm, kmm, kbd, vtd = _proj(xp, w_in[0], tm=PROJ_TM, attn_layouts=True)
    xs = x_sample.reshape(dec_batch, D_MODEL)
    qm_s, km_s, vm_s, zm_s, qd_s, kd_s, vd_s, zd_s, gt_s = _proj(xs, w_in[0], tm=dec_batch, attn_layouts=False)
    cmk3 = cache_moba_k.reshape(n_phys, ROWS_PG, HEAD_DIM)
    cmv3 = cache_moba_v.reshape(n_phys, ROWS_PG, HEAD_DIM)

    def diff_pages(c):
        c = c.reshape(n_phys, PAGE_SIZE, DIFF_KV_HEADS, 2, HEAD_DIM)
        return jnp.transpose(c, (0, 1, 3, 2, 4)).reshape(n_phys, ROWS_PG, HEAD_DIM)

    cdk3 = diff_pages(cache_diff_k)
    cdv3 = diff_pages(cache_diff_v)

    def pair_rows(a):
        a = jnp.transpose(a.reshape(dec_batch, 2, DIFF_KV_HEADS, HEAD_DIM), (0, 2, 1, 3))
        a = a.reshape(dec_batch, DIFF_KV_HEADS, 1, 2, HEAD_DIM)
        return jnp.broadcast_to(a, (dec_batch, DIFF_KV_HEADS, 2, 2, HEAD_DIM)).reshape(dec_batch, 8, HEAD_DIM)

    def diff_out(a, *lead):
        a = a.reshape(lead + (2, DIFF_KV_HEADS, HEAD_DIM))
        a = jnp.swapaxes(a, -3, -2)
        return a.reshape((DEPTH,) + lead + (DIFF_KV_HEADS, DIFF_V_DIM))

    qd8 = qd_s.reshape(dec_batch, 8, HEAD_DIM)
    qm8 = qm_s.reshape(dec_batch, 8, HEAD_DIM)
    zd8 = zd_s.reshape(dec_batch, 8, HEAD_DIM)
    zm8 = zm_s.reshape(dec_batch, 8, HEAD_DIM)
    gn8 = diff_norm_g.reshape(8, HEAD_DIM)
    qz = tuple(a.reshape(batch, seq, 1024) for a in (qm, zm, qd, zd))
    prompt_ops = (kbm, vtm, kmm, kbd, vtd, diff_norm_g.reshape(DIFF_KV_HEADS, 2, DIFF_V_DIM)) + qz + qz
    ad_s, sel, am0, ad0, am1, ad1 = _dec1_attn(page_table, qd8, qm8, pair_rows(kd_s), pair_rows(vd_s), zd8, gn8, lams,
                                               cdk3, cdv3, cmk3, prompt_ops, past_len, batch, seq)
    y_p = _post((am0, am1, ad0, ad1), gt, xp, wmo, wdo, wo, lng, lnb, tm=MOBA_BLOCK)
    sel2 = sel[:, :, :MOBA_TOPK].reshape(dec_batch, MOBA_HEADS * MOBA_TOPK)
    kmn8 = jnp.repeat(km_s.reshape(dec_batch, MOBA_KV_HEADS, HEAD_DIM), 2, axis=1)
    vmn8 = jnp.repeat(vm_s.reshape(dec_batch, MOBA_KV_HEADS, HEAD_DIM), 2, axis=1)
    am_s = _dec2(page_table, sel2, qm8, kmn8, vmn8, zm8, cmk3, cmv3, past_len)
    y_s = _post((am_s.reshape(dec_batch, 1024), ad_s.reshape(dec_batch, 1024)), gt_s, xs,
                wmo, wdo, wo, lng, lnb, tm=dec_batch)

    return (y_p.reshape(batch, seq, D_MODEL), y_s.reshape(dec_batch, 1, D_MODEL),
            km.reshape(DEPTH, batch, seq, MOBA_KV_HEADS, HEAD_DIM),
            vm.reshape(DEPTH, batch, seq, MOBA_KV_HEADS, HEAD_DIM),
            diff_out(kd, batch, seq), diff_out(vd, batch, seq),
            km_s.reshape(DEPTH, dec_batch, 1, MOBA_KV_HEADS, HEAD_DIM),
            vm_s.reshape(DEPTH, dec_batch, 1, MOBA_KV_HEADS, HEAD_DIM),
            diff_out(kd_s, dec_batch, 1), diff_out(vd_s, dec_batch, 1))
```

```python
import functools
import math

import jax
import jax.numpy as jnp
from jax import lax
from jax.experimental import pallas as pl
from jax.experimental.pallas import tpu as pltpu

F32 = jnp.float32
BF16 = jnp.bfloat16

D_MODEL = 2048
HEAD_DIM = 128
PAGE_SIZE = 128
MOBA_HEADS = 8
MOBA_KV_HEADS = 4
MOBA_BLOCK = 256
MOBA_TOPK = 3
DIFF_HEADS = 4
DIFF_KV_HEADS = 2
DIFF_V_DIM = 2 * HEAD_DIM
N_BRANCH = 2
DEPTH = 1
ALPHA = (2 * DEPTH) ** 0.25
LN_EPS = 1e-5
RMS_EPS = 1e-5
NEG_INF = -1e30
SCALE = HEAD_DIM ** -0.5
LAM0 = 0.8 - 0.6 * math.exp(-0.3 * 0)

VMEM_LIMIT_V7X = 56 * 1024 * 1024

PROJ_TN = 1024
PROJ_TM = 512

_NT = (((1,), (1,)), ((), ()))


def _pow2_neg(k):
    return lax.bitcast_convert_type((127 - k) << 23, F32)


def _silu(v):
    return v * (1.0 / (1.0 + jnp.exp(-v)))


def _sigmoid(v):
    return 1.0 / (1.0 + jnp.exp(-v))


def _proj_kernel(x_ref, w_ref, qm_ref, km_ref, vm_ref, zm_ref, qd_ref, kd_ref, vd_ref, zd_ref, gt_ref, *rest):
    attn_refs, wb_sc = rest[:-1], rest[-1]
    n = pl.program_id(0)
    tm = x_ref.shape[0]
    blk = MOBA_BLOCK

    @pl.when(pl.program_id(1) == 0)
    def _():
        wb_sc[...] = w_ref[...].astype(BF16)

    def tile():
        return jnp.dot(x_ref[...], wb_sc[...], preferred_element_type=F32)

    half = PROJ_TN // 2

    @pl.when(n == 0)
    def _():
        qm_ref[...] = tile().astype(BF16)

    @pl.when(n == 1)
    def _():
        acc = tile()
        for h in range(MOBA_KV_HEADS):
            rows_h = pl.ds(h, tm, stride=MOBA_KV_HEADS)
            km_ref[rows_h, :] = acc[:, h * HEAD_DIM:(h + 1) * HEAD_DIM]
            vm_ref[rows_h, :] = acc[:, half + h * HEAD_DIM:half + (h + 1) * HEAD_DIM]
        if attn_refs:
            kbm_ref, vtm_ref, kmm_ref = attn_refs[:3]
            for h in range(MOBA_KV_HEADS):
                k_h = acc[:, h * HEAD_DIM:(h + 1) * HEAD_DIM]
                kbm_ref[h] = k_h.astype(BF16)
                kmm_ref[0, h] = jnp.sum(k_h.reshape(tm // blk, blk, HEAD_DIM), axis=1) * (1.0 / blk)
                for r in range(tm // blk):
                    v_hr = acc[r * blk:(r + 1) * blk, half + h * HEAD_DIM:half + (h + 1) * HEAD_DIM]
                    vtm_ref[h, r] = v_hr.T.astype(BF16)

    @pl.when(n == 2)
    def _():
        zm_ref[...] = _silu(tile()).astype(BF16)

    @pl.when(n == 3)
    def _():
        qd_ref[...] = tile().astype(BF16)

    @pl.when(n == 4)
    def _():
        acc = tile()
        for hk in range(DIFF_KV_HEADS):
            for c in range(2):
                rows_c = pl.ds(c * DIFF_KV_HEADS + hk, tm, stride=2 * DIFF_KV_HEADS)
                lo = hk * DIFF_V_DIM + c * HEAD_DIM
                kd_ref[rows_c, :] = acc[:, lo:lo + HEAD_DIM]
                vd_ref[rows_c, :] = acc[:, half + lo:half + lo + HEAD_DIM]
        if attn_refs:
            kbd_ref, vtd_ref = attn_refs[3:]
            for hk in range(DIFF_KV_HEADS):
                kbd_ref[hk] = acc[:, hk * DIFF_V_DIM:(hk + 1) * DIFF_V_DIM].astype(BF16)
                for r in range(tm // blk):
                    v_hr = acc[r * blk:(r + 1) * blk, half + hk * DIFF_V_DIM:half + (hk + 1) * DIFF_V_DIM]
                    vtd_ref[hk, r] = v_hr.T.astype(BF16)

    @pl.when(n == 5)
    def _():
        zd_ref[...] = _silu(tile()).astype(BF16)

    @pl.when(n >= 6)
    def _():
        gt_ref[...] = _sigmoid(tile()).astype(BF16)


def _proj(x2d, w, tm, attn_layouts):
    m_rows = x2d.shape[0]
    n_tiles = w.shape[1] // PROJ_TN
    m_last = m_rows // tm - 1

    def out_map(lo, hi):
        return lambda n, m: (jnp.where(n < lo, 0, jnp.where(n >= hi, m_last, m)), jnp.clip(n - lo, 0, hi - lo - 1))

    def row_tile(lo):
        return lambda n, m: jnp.where(n < lo, 0, jnp.where(n > lo, m_last, m))

    out_shape = (
        jax.ShapeDtypeStruct((m_rows, 1024), BF16),
        jax.ShapeDtypeStruct((m_rows * 4, HEAD_DIM), F32),
        jax.ShapeDtypeStruct((m_rows * 4, HEAD_DIM), F32),
        jax.ShapeDtypeStruct((m_rows, 1024), BF16),
        jax.ShapeDtypeStruct((m_rows, 1024), BF16),
        jax.ShapeDtypeStruct((m_rows * 4, HEAD_DIM), F32),
        jax.ShapeDtypeStruct((m_rows * 4, HEAD_DIM), F32),
        jax.ShapeDtypeStruct((m_rows, 1024), BF16),
        jax.ShapeDtypeStruct((m_rows, 4096), BF16),
    )
    out_specs = (
        pl.BlockSpec((tm, PROJ_TN), out_map(0, 1)),
        pl.BlockSpec((tm * 4, HEAD_DIM), out_map(1, 2)),
        pl.BlockSpec((tm * 4, HEAD_DIM), out_map(1, 2)),
        pl.BlockSpec((tm, PROJ_TN), out_map(2, 3)),
        pl.BlockSpec((tm, PROJ_TN), out_map(3, 4)),
        pl.BlockSpec((tm * 4, HEAD_DIM), out_map(4, 5)),
        pl.BlockSpec((tm * 4, HEAD_DIM), out_map(4, 5)),
        pl.BlockSpec((tm, PROJ_TN), out_map(5, 6)),
        pl.BlockSpec((tm, PROJ_TN), out_map(6, 10)),
    )
    if attn_layouts:
        blk = MOBA_BLOCK
        rt = tm // blk
        tile_m, tile_d = row_tile(1), row_tile(4)
        out_shape += (
            jax.ShapeDtypeStruct((MOBA_KV_HEADS, m_rows, HEAD_DIM), BF16),
            jax.ShapeDtypeStruct((MOBA_KV_HEADS, m_rows // blk, HEAD_DIM, blk), BF16),
            jax.ShapeDtypeStruct((m_rows // tm, MOBA_KV_HEADS, rt, HEAD_DIM), F32),
            jax.ShapeDtypeStruct((DIFF_KV_HEADS, m_rows, DIFF_V_DIM), BF16),
            jax.ShapeDtypeStruct((DIFF_KV_HEADS, m_rows // blk, DIFF_V_DIM, blk), BF16),
        )
        out_specs += (
            pl.BlockSpec((MOBA_KV_HEADS, tm, HEAD_DIM), lambda n, m: (0, tile_m(n, m), 0)),
            pl.BlockSpec((MOBA_KV_HEADS, rt, HEAD_DIM, blk), lambda n, m: (0, tile_m(n, m), 0, 0)),
            pl.BlockSpec((1, MOBA_KV_HEADS, rt, HEAD_DIM), lambda n, m: (tile_m(n, m), 0, 0, 0)),
            pl.BlockSpec((DIFF_KV_HEADS, tm, DIFF_V_DIM), lambda n, m: (0, tile_d(n, m), 0)),
            pl.BlockSpec((DIFF_KV_HEADS, rt, DIFF_V_DIM, blk), lambda n, m: (0, tile_d(n, m), 0, 0)),
        )
    return pl.pallas_call(
        _proj_kernel,
        grid=(n_tiles, m_rows // tm),
        in_specs=[pl.BlockSpec((tm, D_MODEL), lambda n, m: (m, 0)),
                  pl.BlockSpec((D_MODEL, PROJ_TN), lambda n, m: (0, n))],
        out_specs=out_specs,
        out_shape=out_shape,
        scratch_shapes=[pltpu.VMEM((D_MODEL, PROJ_TN), BF16)],
        compiler_params=pltpu.CompilerParams(
            dimension_semantics=("arbitrary", "arbitrary"), vmem_limit_bytes=VMEM_LIMIT_V7X),
        name="proj",
    )(x2d, w)


LOG2E = 1.4426950408889634
SCALE_L2 = SCALE * LOG2E
MASKED = 1e30


def _alibi_tiles_t(blk, qlen, q_off):
    c_i = lax.broadcasted_iota(jnp.int32, (blk, 2 * qlen), 0)
    q_i = lax.broadcasted_iota(jnp.int32, (blk, 2 * qlen), 1)
    g = q_i // qlen
    rel = (q_off + q_i - g * qlen - c_i).astype(F32)
    return rel, g[:1], rel >= 0.0


def _slope_row_l2(g_row, head0, slope_mul):
    return _pow2_neg(slope_mul * (head0 + g_row + 1)) * LOG2E


def _lambda_full(lq1_ref, lk1_ref, lq2_ref, lk2_ref):
    a = jnp.sum(lq1_ref[...] * lk1_ref[...], axis=1, keepdims=True)
    b = jnp.sum(lq2_ref[...] * lk2_ref[...], axis=1, keepdims=True)
    return jnp.exp(a) - jnp.exp(b) + LAM0


def _moba_prompt_steps(j, q_ref, kb_ref, vt_ref, km_ref, z_ref, o_ref, bias_sc, acc_sc, st_sc, geom):
    blk = MOBA_BLOCK
    n_kv = MOBA_KV_HEADS
    rel, g_row, causal = geom
    qlen = rel.shape[1] // 2
    env = {}

    def start():
        qss, ranks = [], []
        for hk in range(n_kv):
            qj = q_ref[0, :, hk * 2 * HEAD_DIM:(hk + 1) * 2 * HEAD_DIM]
            qs = jnp.concatenate([qj[:, :HEAD_DIM], qj[:, HEAD_DIM:]], axis=0)
            kmean = jnp.concatenate([km_ref[t, hk] for t in range(km_ref.shape[0])], axis=0)
            kmean_hi = kmean.astype(BF16)
            kmean_lo = (kmean - kmean_hi.astype(F32)).astype(BF16)
            gate = (lax.dot_general(kmean_hi, qs, _NT, preferred_element_type=F32)
                    + lax.dot_general(kmean_lo, qs, _NT, preferred_element_type=F32))
            sub = lax.broadcasted_iota(jnp.int32, gate.shape, 0)
            gate = jnp.where(sub < j, gate, NEG_INF)
            rank = jnp.zeros(gate.shape, F32)
            for m in range(gate.shape[0]):
                gm = gate[m:m + 1, :]
                ge = jnp.where(gm >= gate, 1.0, 0.0)
                gt = jnp.where(gm > gate, 1.0, 0.0)
                rank = rank + jnp.where(sub > m, ge, gt)
            k_own = kb_ref[hk, pl.ds(pl.multiple_of(j * blk, blk), blk), :]
            u = lax.dot_general(k_own, qs, _NT, preferred_element_type=F32) * SCALE_L2 - bias_sc[hk]
            u = jnp.where(causal, u, -MASKED)
            m0 = jnp.max(u, axis=0, keepdims=True)
            p = jnp.exp2(u - m0)
            st_sc[2 * hk:2 * hk + 1, :] = m0
            st_sc[2 * hk + 1:2 * hk + 2, :] = jnp.sum(p, axis=0, keepdims=True)
            acc_sc[hk] = jnp.dot(vt_ref[hk, j], p.astype(BF16), preferred_element_type=F32)
            qss.append(qs)
            ranks.append(rank)
        env["qss"], env["ranks"] = qss, ranks

    def piece(n):
        dist_blocks = (j - n).astype(F32)
        for hk in range(n_kv):
            m_run = st_sc[2 * hk:2 * hk + 1, :]
            l_run = st_sc[2 * hk + 1:2 * hk + 2, :]
            k_n = kb_ref[hk, pl.ds(pl.multiple_of(n * blk, blk), blk), :]
            u = lax.dot_general(k_n, env["qss"][hk], _NT, preferred_element_type=F32) * SCALE_L2 - bias_sc[hk]
            rank = env["ranks"][hk]
            sub = lax.broadcasted_iota(jnp.int32, rank.shape, 0)
            rk = jnp.sum(jnp.where(sub == n, rank, 0.0), axis=0, keepdims=True)
            slope_blk = _slope_row_l2(g_row, 2 * hk, 1) * float(blk)
            rowc = jnp.where(rk < float(MOBA_TOPK), slope_blk * dist_blocks, MASKED)
            m_new = jnp.maximum(m_run, jnp.max(u, axis=0, keepdims=True) - rowc)
            alpha = jnp.exp2(m_run - m_new)
            p = jnp.exp2(u - (rowc + m_new))
            st_sc[2 * hk:2 * hk + 1, :] = m_new
            st_sc[2 * hk + 1:2 * hk + 2, :] = alpha * l_run + jnp.sum(p, axis=0, keepdims=True)
            acc_sc[hk] = alpha * acc_sc[hk] + jnp.dot(vt_ref[hk, n], p.astype(BF16), preferred_element_type=F32)

    def finish():
        for hk in range(n_kv):
            ot = acc_sc[hk] / st_sc[2 * hk + 1:2 * hk + 2, :]
            o2 = jnp.concatenate([ot[:, :qlen].T, ot[:, qlen:].T], axis=1)
            cols = slice(hk * 2 * HEAD_DIM, (hk + 1) * 2 * HEAD_DIM)
            o_ref[0, :, cols] = (o2 * z_ref[0, :, cols].astype(F32)).astype(BF16)

    return start, piece, finish


def _diff_prompt_steps(j, lam, q_ref, kb_ref, vt_ref, z_ref, gn_ref, o_ref, bias_sc, acc_sc, st_sc, geom):
    blk = MOBA_BLOCK
    n_kv = DIFF_KV_HEADS
    rel, g_row, causal = geom
    qlen = rel.shape[1] // 2
    chains = [(hk, mi) for hk in range(n_kv) for mi in range(2)]
    env = {}

    def scores(hk, mi, n):
        k_n = kb_ref[hk, pl.ds(pl.multiple_of(n * blk, blk), blk), mi * HEAD_DIM:(mi + 1) * HEAD_DIM]
        return lax.dot_general(k_n, env["qss"][hk, mi], _NT, preferred_element_type=F32) * SCALE_L2 - bias_sc[hk]

    def start():
        qss = {}
        for hk in range(n_kv):
            qj = q_ref[0, :, hk * 4 * HEAD_DIM:(hk + 1) * 4 * HEAD_DIM]
            for mi in range(2):
                qss[hk, mi] = jnp.concatenate([qj[:, mi * HEAD_DIM:(mi + 1) * HEAD_DIM],
                                               qj[:, (2 + mi) * HEAD_DIM:(3 + mi) * HEAD_DIM]], axis=0)
        env["qss"] = qss
        for ci, (hk, mi) in enumerate(chains):
            u = jnp.where(causal, scores(hk, mi, j), -MASKED)
            m0 = jnp.max(u, axis=0, keepdims=True)
            p = jnp.exp2(u - m0)
            st_sc[2 * ci:2 * ci + 1, :] = m0
            st_sc[2 * ci + 1:2 * ci + 2, :] = jnp.sum(p, axis=0, keepdims=True)
            acc_sc[ci] = jnp.dot(vt_ref[hk, j], p.astype(BF16), preferred_element_type=F32)

    def piece(n):
        dist_blocks = (j - n).astype(F32)
        for ci, (hk, mi) in enumerate(chains):
            m_run = st_sc[2 * ci:2 * ci + 1, :]
            l_run = st_sc[2 * ci + 1:2 * ci + 2, :]
            rowc = _slope_row_l2(g_row, 2 * hk, 2) * float(blk) * dist_blocks
            u = scores(hk, mi, n)
            m_new = jnp.maximum(m_run, jnp.max(u, axis=0, keepdims=True) - rowc)
            alpha = jnp.exp2(m_run - m_new)
            p = jnp.exp2(u - (rowc + m_new))
            st_sc[2 * ci:2 * ci + 1, :] = m_new
            st_sc[2 * ci + 1:2 * ci + 2, :] = alpha * l_run + jnp.sum(p, axis=0, keepdims=True)
            acc_sc[ci] = alpha * acc_sc[ci] + jnp.dot(vt_ref[hk, n], p.astype(BF16), preferred_element_type=F32)

    def finish():
        for hk in range(n_kv):
            c1, c2 = 2 * hk, 2 * hk + 1
            ot = (acc_sc[c1] / st_sc[2 * c1 + 1:2 * c1 + 2, :]
                  - lam * (acc_sc[c2] / st_sc[2 * c2 + 1:2 * c2 + 2, :]))
            ms = jnp.mean(ot * ot, axis=0, keepdims=True)
            ot = ot * lax.rsqrt(ms + RMS_EPS)
            gn = gn_ref[hk]
            o2 = jnp.concatenate([ot[:, :qlen].T * gn[0:1, :], ot[:, qlen:].T * gn[1:2, :]], axis=1) * (1.0 - LAM0)
            cols = slice(hk * 2 * DIFF_V_DIM, (hk + 1) * 2 * DIFF_V_DIM)
            o_ref[0, :, cols] = (o2 * z_ref[0, :, cols].astype(F32)).astype(BF16)

    return start, piece, finish


ROWS_PG = PAGE_SIZE * 4
DEC1_PAGES_PER_CHUNK = 8
PAGES_PER_BLOCK = MOBA_BLOCK // PAGE_SIZE


def _pair_lo(x, even):
    return jnp.where(even, x, pltpu.roll(x, 1, axis=0))


def _pair_hi(x, even):
    return jnp.where(even, pltpu.roll(x, x.shape[0] - 1, axis=0), x)


def _dec1_attn_kernel(pt_ref, qd_ref, qm_ref, kdn_ref, vdn_ref, zd_ref, gn_ref, lq1_ref, lk1_ref, lq2_ref, lk2_ref,
                      ck_hbm, cv_hbm, cmk_hbm,
                      kbm_ref, vtm_ref, kmm_ref, kbd_ref, vtd_ref, gnp_ref,
                      qm0_ref, zm0_ref, qd0_ref, zd0_ref, qm1_ref, zm1_ref, qd1_ref, zd1_ref,
                      ad_ref, sel_ref, am0_ref, ad0_ref, am1_ref, ad1_ref,
                      kbuf, vbuf, mbuf, sems, ks_sc, bias_m, bias_d, acc_m, acc_d, st_m, st_d,
                      *, n_pages, past_len, n_blk):
    b = pl.program_id(0)
    n_b = pl.num_programs(0)
    j = b % n_blk
    j_sub = (j, n_blk - 1 - j)
    qlen = MOBA_BLOCK // 2
    ppc = DEC1_PAGES_PER_CHUNK
    n_chunks = n_pages // ppc
    rows = ppc * ROWS_PG
    streams = ((ck_hbm, kbuf), (cv_hbm, vbuf), (cmk_hbm, mbuf))

    def chunk_copies(bb, c, slot):
        out = []
        for k in range(ppc):
            page = pt_ref[bb, c * ppc + k]
            for si, (src, dst) in enumerate(streams):
                out.append(pltpu.make_async_copy(
                    src.at[page], dst.at[slot, pl.ds(k * ROWS_PG, ROWS_PG), :], sems.at[si, slot]))
        return out

    geoms = [_alibi_tiles_t(MOBA_BLOCK, qlen, t * qlen) for t in range(2)]

    @pl.when(b == 0)
    def _():
        for cp in chunk_copies(0, 0, 0) + chunk_copies(0, 1, 1):
            cp.start()
        for t in range(2):
            for hk in range(MOBA_KV_HEADS):
                bias_m[t, hk] = geoms[t][0] * _slope_row_l2(geoms[t][1], 2 * hk, 1)
            for hk in range(DIFF_KV_HEADS):
                bias_d[t, hk] = geoms[t][0] * _slope_row_l2(geoms[t][1], 2 * hk, 2)

    lam = _lambda_full(lq1_ref, lk1_ref, lq2_ref, lk2_ref)
    moba_io = ((qm0_ref, zm0_ref, am0_ref), (qm1_ref, zm1_ref, am1_ref))
    diff_io = ((qd0_ref, zd0_ref, ad0_ref), (qd1_ref, zd1_ref, ad1_ref))
    moba, diff = [], []
    for t in range(2):
        q_r, z_r, o_r = moba_io[t]
        moba.append(_moba_prompt_steps(j_sub[t], q_r, kbm_ref, vtm_ref, kmm_ref, z_r, o_r,
                                       bias_m.at[t], acc_m.at[t], st_m.at[t], geoms[t]))
        q_r, z_r, o_r = diff_io[t]
        diff.append(_diff_prompt_steps(j_sub[t], lam, q_r, kbd_ref, vtd_ref, z_r, gnp_ref, o_r,
                                       bias_d.at[t], acc_d.at[t], st_d.at[t], geoms[t]))
    for t in range(2):
        moba[t][0]()
        diff[t][0]()

    q8 = qd_ref[0]
    row = lax.broadcasted_iota(jnp.int32, (8, rows), 0)
    col = lax.broadcasted_iota(jnp.int32, (8, rows), 1)
    row1 = lax.broadcasted_iota(jnp.int32, (8, HEAD_DIM), 0)
    even = (row & 1) == 0
    even1 = (row1 & 1) == 0
    valid = (col & 3) == ((row & 1) * 2 + (row >> 2))
    slope = _pow2_neg(2 * ((row >> 1) + 1))
    posf = (col >> 2).astype(F32)

    def chunk_compute(c, slot, carry):
        m_old, l_old, acc = carry
        s = lax.dot_general(q8, kbuf[slot].astype(BF16), _NT, preferred_element_type=F32) * SCALE
        base = jnp.asarray(past_len - c * (ppc * PAGE_SIZE)).astype(F32)
        s = jnp.where(valid, s - slope * (base - posf), NEG_INF)
        m_new = jnp.maximum(m_old, jnp.max(s, axis=1, keepdims=True))
        alpha = jnp.exp(m_old - m_new)
        p = jnp.where(valid, jnp.exp(s - m_new[:, :1]), 0.0)
        l_new = alpha * l_old + jnp.sum(p, axis=1, keepdims=True)
        p_r = pltpu.roll(p, 2, axis=1)
        p_l = pltpu.roll(p, rows - 2, axis=1)
        w1 = jnp.where(even, p, pltpu.roll(p_r, 1, axis=0))
        w2 = jnp.where(even, pltpu.roll(p_l, 7, axis=0), p)
        w = jnp.concatenate([w1, w2], axis=0).astype(BF16)
        a16 = jnp.concatenate([_pair_lo(alpha, even1), _pair_hi(alpha, even1)], axis=0)
        acc = a16 * acc + jnp.dot(w, vbuf[slot].astype(BF16), preferred_element_type=F32)
        bpc = ppc // PAGES_PER_BLOCK
        ksum = jnp.sum(mbuf[slot].reshape(bpc, PAGES_PER_BLOCK * ROWS_PG // 8, 8, HEAD_DIM), axis=1)
        ks_sc[pl.ds(c * bpc, bpc)] = ksum
        return m_new, l_new, acc

    def pair_body(c2, carry):
        for u in range(2):
            c = 2 * c2 + u
            for cp in chunk_copies(b, c, u):
                cp.wait()
            carry = chunk_compute(c, u, carry)
            in_b = c + 2 < n_chunks

            @pl.when(in_b | (b + 1 < n_b))
            def _():
                for cp in chunk_copies(jnp.where(in_b, b, b + 1), jnp.where(in_b, c + 2, c + 2 - n_chunks), u):
                    cp.start()

            for t in range(2):
                pl.when(c2 < j_sub[t])(functools.partial((moba if u == 0 else diff)[t][1], c2))
        return carry

    init = (jnp.full((8, HEAD_DIM), NEG_INF, F32), jnp.zeros((8, HEAD_DIM), F32), jnp.zeros((16, HEAD_DIM), F32))
    m_o, l_o, acc = lax.fori_loop(0, n_chunks // 2, pair_body, init)
    for t in range(2):
        moba[t][2]()
        diff[t][2]()

    qf = q8.astype(F32)
    s_new = jnp.sum(qf * kdn_ref[0], axis=1, keepdims=True) * SCALE
    m_f = jnp.maximum(m_o, s_new)
    a_f = jnp.exp(m_o - m_f)
    p_new = jnp.exp(s_new - m_f)
    l_f = a_f * l_o + p_new
    vdn = vdn_ref[0]
    acc1 = _pair_lo(a_f, even1) * acc[:8] + _pair_lo(p_new, even1) * vdn
    acc2 = _pair_hi(a_f, even1) * acc[8:] + _pair_hi(p_new, even1) * vdn
    o = acc1 / _pair_lo(l_f, even1) - lam * (acc2 / _pair_hi(l_f, even1))
    ss = jnp.broadcast_to(jnp.sum(o * o, axis=1, keepdims=True), (8, HEAD_DIM))
    ss = ss + jnp.where(even1, pltpu.roll(ss, 7, axis=0), pltpu.roll(ss, 1, axis=0))
    o = o * lax.rsqrt(ss * (1.0 / DIFF_V_DIM) + RMS_EPS) * gn_ref[...] * (1.0 - LAM0)
    ad_ref[0] = (o * zd_ref[0].astype(F32)).astype(BF16)

    n_cols = ks_sc.shape[0] * 8
    ks = ks_sc[...].reshape(n_cols, HEAD_DIM)
    ks_hi = ks.astype(BF16)
    ks_lo = (ks - ks_hi.astype(F32)).astype(BF16)
    qm8 = qm_ref[0]
    g_all = (lax.dot_general(qm8, ks_hi, _NT, preferred_element_type=F32)
             + lax.dot_general(qm8, ks_lo, _NT, preferred_element_type=F32)) * (1.0 / MOBA_BLOCK)
    g2 = g_all + pltpu.roll(g_all, n_cols - 4, axis=1)
    rw = lax.broadcasted_iota(jnp.int32, (8, n_cols), 0)
    cl = lax.broadcasted_iota(jnp.int32, (8, n_cols), 1)
    clf = cl.astype(F32)
    gm = jnp.where((cl & 7) == (rw >> 1), g2, NEG_INF)
    lane = lax.broadcasted_iota(jnp.int32, (8, HEAD_DIM), 1)
    sel = jnp.zeros((8, HEAD_DIM), jnp.int32)
    for t in range(MOBA_TOPK):
        mx = jnp.max(gm, axis=1, keepdims=True)
        idx = jnp.min(jnp.where(gm == mx, clf, float(n_cols)), axis=1, keepdims=True)
        gm = jnp.where(clf == idx, NEG_INF, gm)
        sel = jnp.where(lane == t, idx.astype(jnp.int32) >> 3, sel)
    sel_ref[0] = sel


def _dec1_attn(page_table, qd8, qm8, kdn8, vdn8, zd8, gn8, lams, ck3, cv3, cmk3, prompt_ops, past_len, batch, seq):
    dec_batch, n_pages = page_table.shape
    blk = MOBA_BLOCK
    n_blk = seq // blk
    n_chunks = n_pages // DEC1_PAGES_PER_CHUNK
    assert dec_batch == batch * n_blk and n_chunks == 2 * n_blk and DEC1_PAGES_PER_CHUNK % PAGES_PER_BLOCK == 0
    n_blocks = n_pages // PAGES_PER_BLOCK
    rows = DEC1_PAGES_PER_CHUNK * ROWS_PG
    rt = PROJ_TM // blk
    row8 = lambda: pl.BlockSpec((1, 8, HEAD_DIM), lambda b, pt: (b, 0, 0))
    vec = pl.BlockSpec((1, HEAD_DIM), lambda b, pt: (0, 0))
    hbm = lambda: pl.BlockSpec(memory_space=pl.ANY)
    qlen = blk // 2
    half_in = (lambda: pl.BlockSpec((1, qlen, 1024), lambda b, pt: (b // n_blk, 2 * (b % n_blk), 0)),
               lambda: pl.BlockSpec((1, qlen, 1024), lambda b, pt: (b // n_blk, 2 * (n_blk - 1 - b % n_blk) + 1, 0)))
    half_out = (lambda: pl.BlockSpec((1, qlen, 1024), lambda b, pt: (b // n_blk, b % n_blk, 0)),
                lambda: pl.BlockSpec((1, qlen, 1024), lambda b, pt: (b // n_blk, n_blk - 1 - b % n_blk, 0)))
    grid_spec = pltpu.PrefetchScalarGridSpec(
        num_scalar_prefetch=1,
        grid=(dec_batch,),
        in_specs=[row8(), row8(), row8(), row8(), row8(),
                  pl.BlockSpec((8, HEAD_DIM), lambda b, pt: (0, 0)),
                  vec, vec, vec, vec,
                  hbm(), hbm(), hbm(),
                  pl.BlockSpec((MOBA_KV_HEADS, seq, HEAD_DIM), lambda b, pt: (0, b // n_blk, 0)),
                  pl.BlockSpec((MOBA_KV_HEADS, n_blk, HEAD_DIM, blk), lambda b, pt: (0, b // n_blk, 0, 0)),
                  pl.BlockSpec((n_blk // rt, MOBA_KV_HEADS, rt, HEAD_DIM), lambda b, pt: (b // n_blk, 0, 0, 0)),
                  pl.BlockSpec((DIFF_KV_HEADS, seq, DIFF_V_DIM), lambda b, pt: (0, b // n_blk, 0)),
                  pl.BlockSpec((DIFF_KV_HEADS, n_blk, DIFF_V_DIM, blk), lambda b, pt: (0, b // n_blk, 0, 0)),
                  pl.BlockSpec((DIFF_KV_HEADS, 2, DIFF_V_DIM), lambda b, pt: (0, 0, 0)),
                  half_in[0](), half_in[0](), half_in[0](), half_in[0](),
                  half_in[1](), half_in[1](), half_in[1](), half_in[1]()],
        out_specs=(pl.BlockSpec((1, 8, HEAD_DIM), lambda b, pt: (b, 0, 0)),
                   pl.BlockSpec((1, 8, HEAD_DIM), lambda b, pt: (b, 0, 0)),
                   half_out[0](), half_out[0](), half_out[1](), half_out[1]()),
        scratch_shapes=[pltpu.VMEM((2, rows, HEAD_DIM), F32), pltpu.VMEM((2, rows, HEAD_DIM), F32),
                        pltpu.VMEM((2, rows, HEAD_DIM), F32), pltpu.SemaphoreType.DMA((3, 2)),
                        pltpu.VMEM((n_blocks, 8, HEAD_DIM), F32),
                        pltpu.VMEM((2, MOBA_KV_HEADS, blk, 2 * qlen), F32),
                        pltpu.VMEM((2, DIFF_KV_HEADS, blk, 2 * qlen), F32),
                        pltpu.VMEM((2, MOBA_KV_HEADS, HEAD_DIM, 2 * qlen), F32),
                        pltpu.VMEM((2, 2 * DIFF_KV_HEADS, DIFF_V_DIM, 2 * qlen), F32),
                        pltpu.VMEM((2, 2 * MOBA_KV_HEADS, 2 * qlen), F32),
                        pltpu.VMEM((2, 4 * DIFF_KV_HEADS, 2 * qlen), F32)],
    )
    return pl.pallas_call(
        functools.partial(_dec1_attn_kernel, n_pages=n_pages, past_len=past_len, n_blk=n_blk),
        grid_spec=grid_spec,
        out_shape=(jax.ShapeDtypeStruct((dec_batch, 8, HEAD_DIM), BF16),
                   jax.ShapeDtypeStruct((dec_batch, 8, HEAD_DIM), jnp.int32),
                   ) + (jax.ShapeDtypeStruct((batch, seq // 2, 1024), BF16),) * 4,
        compiler_params=pltpu.CompilerParams(
            dimension_semantics=("arbitrary",), vmem_limit_bytes=VMEM_LIMIT_V7X),
        name="dec1_attn",
    )(page_table, qd8, qm8, kdn8, vdn8, zd8, gn8, *lams, ck3, cv3, cmk3, *prompt_ops)


def _dec2_kernel(pt_ref, sel_ref, qm_ref, kmn_ref, vmn_ref, zm_ref, ck_hbm, cv_hbm, am_ref, kbuf, vbuf, sems, o_sc,
                 *, past_len):
    g = pl.program_id(0)
    n_g = pl.num_programs(0)
    hk = g % MOBA_KV_HEADS
    slot = g % 2
    blk_rows = PAGES_PER_BLOCK * ROWS_PG
    head_rows = MOBA_TOPK * blk_rows
    rows = 2 * head_rows

    def step_copies(gg, sl):
        bb = gg // MOBA_KV_HEADS
        kk = gg % MOBA_KV_HEADS
        out = []
        for gi in range(2):
            for t in range(MOBA_TOPK):
                blk = sel_ref[bb, (2 * kk + gi) * MOBA_TOPK + t]
                for f in range(PAGES_PER_BLOCK):
                    page = pt_ref[bb, blk * PAGES_PER_BLOCK + f]
                    dst = pl.ds(gi * head_rows + (t * PAGES_PER_BLOCK + f) * ROWS_PG, ROWS_PG)
                    out.append(pltpu.make_async_copy(ck_hbm.at[page], kbuf.at[sl, dst, :], sems.at[0, sl]))
                    out.append(pltpu.make_async_copy(cv_hbm.at[page], vbuf.at[sl, dst, :], sems.at[1, sl]))
        return out

    @pl.when(g == 0)
    def _():
        for cp in step_copies(0, 0):
            cp.start()

    @pl.when(g + 1 < n_g)
    def _():
        for cp in step_copies(g + 1, 1 - slot):
            cp.start()

    for cp in step_copies(g, slot):
        cp.wait()

    q8 = qm_ref[0]
    row = lax.broadcasted_iota(jnp.int32, (8, rows), 0)
    col = lax.broadcasted_iota(jnp.int32, (8, rows), 1)
    row1 = lax.broadcasted_iota(jnp.int32, (8, HEAD_DIM), 0)
    b = g // MOBA_KV_HEADS
    blk_col = jnp.zeros((8, rows), jnp.int32)
    for gi in range(2):
        for t in range(MOBA_TOPK):
            lo = gi * head_rows + t * blk_rows
            seg = (col >= lo) & (col < lo + blk_rows)
            blk_col = jnp.where(seg, sel_ref[b, (2 * hk + gi) * MOBA_TOPK + t], blk_col)
    kpos = blk_col * MOBA_BLOCK + ((col & (blk_rows - 1)) >> 2)
    s = lax.dot_general(q8, kbuf[slot].astype(BF16), _NT, preferred_element_type=F32) * SCALE
    s = s - _pow2_neg(row + 1) * (past_len - kpos).astype(F32)
    head_col = 2 * hk + jnp.where(col >= head_rows, 1, 0)
    valid = (row == head_col) & ((col & 3) == hk)
    s = jnp.where(valid, s, NEG_INF)
    s_new = jnp.sum(q8.astype(F32) * kmn_ref[0], axis=1, keepdims=True) * SCALE
    m = jnp.maximum(jnp.max(s, axis=1, keepdims=True), s_new)
    p = jnp.where(valid, jnp.exp(s - m), 0.0)
    p_new = jnp.exp(s_new - m)
    l = jnp.sum(p, axis=1, keepdims=True) + p_new
    o = (jnp.dot(p.astype(BF16), vbuf[slot].astype(BF16), preferred_element_type=F32) + p_new * vmn_ref[0]) / l

    @pl.when(hk == 0)
    def _():
        o_sc[...] = jnp.zeros(o_sc.shape, F32)

    o_sc[...] = jnp.where((row1 >> 1) == hk, o, o_sc[...])

    @pl.when(hk == MOBA_KV_HEADS - 1)
    def _():
        am_ref[0] = (o_sc[...] * zm_ref[0].astype(F32)).astype(BF16)


def _dec2(page_table, sel, qm8, kmn8, vmn8, zm8, cmk3, cmv3, past_len):
    dec_batch = page_table.shape[0]
    rows = 2 * MOBA_TOPK * PAGES_PER_BLOCK * ROWS_PG
    row8 = lambda: pl.BlockSpec((1, 8, HEAD_DIM), lambda g, pt, sl: (g // MOBA_KV_HEADS, 0, 0))
    hbm = lambda: pl.BlockSpec(memory_space=pl.ANY)
    grid_spec = pltpu.PrefetchScalarGridSpec(
        num_scalar_prefetch=2,
        grid=(dec_batch * MOBA_KV_HEADS,),
        in_specs=[row8(), row8(), row8(), row8(), hbm(), hbm()],
        out_specs=pl.BlockSpec((1, 8, HEAD_DIM), lambda g, pt, sl: (g // MOBA_KV_HEADS, 0, 0)),
        scratch_shapes=[pltpu.VMEM((2, rows, HEAD_DIM), F32), pltpu.VMEM((2, rows, HEAD_DIM), F32),
                        pltpu.SemaphoreType.DMA((2, 2)), pltpu.VMEM((8, HEAD_DIM), F32)],
    )
    return pl.pallas_call(
        functools.partial(_dec2_kernel, past_len=past_len),
        grid_spec=grid_spec,
        out_shape=jax.ShapeDtypeStruct((dec_batch, 8, HEAD_DIM), BF16),
        compiler_params=pltpu.CompilerParams(
            dimension_semantics=("arbitrary",), vmem_limit_bytes=VMEM_LIMIT_V7X),
        name="dec2",
    )(page_table, sel, qm8, kmn8, vmn8, zm8, cmk3, cmv3)


def _post_kernel(*refs):
    n_act = len(refs) - 8
    g_ref, x_ref, wmo_ref, wdo_ref, wo_ref, lng_ref, lnb_ref, y_ref = refs[n_act:]
    if n_act == 2:
        am, ad = refs[0][...], refs[1][...]
    else:
        am = jnp.concatenate([refs[0][0], refs[1][0]], axis=0)
        ad = jnp.concatenate([refs[2][0], refs[3][0]], axis=0)
    ym = jnp.dot(am, wmo_ref[...], preferred_element_type=F32)
    yd = jnp.dot(ad, wdo_ref[...], preferred_element_type=F32)
    g = g_ref[...].astype(F32)
    merged = g[:, :D_MODEL] * ym + g[:, D_MODEL:] * yd
    f = jnp.dot(merged.astype(BF16), wo_ref[...], preferred_element_type=F32)
    u = ALPHA * x_ref[...] + f
    mu = jnp.mean(u, axis=1, keepdims=True)
    d = u - mu
    var = jnp.mean(d * d, axis=1, keepdims=True)
    y_ref[...] = d * lax.rsqrt(var + LN_EPS) * lng_ref[...] + lnb_ref[...]


def _post(acts, gt, x2d, wmo, wdo, wo, lng, lnb, tm):
    m_rows = x2d.shape[0]
    rowblk = lambda w: pl.BlockSpec((tm, w), lambda m: (m, 0))
    full = lambda a: pl.BlockSpec(a.shape, lambda m: (0, 0))
    if len(acts) == 2:
        act_specs = [rowblk(1024), rowblk(1024)]
    else:
        assert tm == MOBA_BLOCK
        per_batch = acts[0].shape[1] // (tm // 2)
        act_specs = [pl.BlockSpec((1, tm // 2, 1024), lambda m: (m // per_batch, m % per_batch, 0))] * 4
    return pl.pallas_call(
        _post_kernel,
        grid=(m_rows // tm,),
        in_specs=act_specs + [rowblk(N_BRANCH * D_MODEL), rowblk(D_MODEL),
                              full(wmo), full(wdo), full(wo), full(lng), full(lnb)],
        out_specs=rowblk(D_MODEL),
        out_shape=jax.ShapeDtypeStruct((m_rows, D_MODEL), F32),
        compiler_params=pltpu.CompilerParams(
            dimension_semantics=("parallel",), vmem_limit_bytes=VMEM_LIMIT_V7X),
        name="post",
    )(*acts, gt, x2d, wmo, wdo, wo, lng, lnb)


def kernel(x_prompt, x_sample, cache_moba_k, cache_moba_v, cache_diff_k, cache_diff_v, page_table, w_in, w_moba_o, w_diff_o, w_out, lambda_q1, lambda_k1, lambda_q2, lambda_k2, diff_norm_g, ln_g, ln_b):
    assert w_in.shape[0] == DEPTH == 1
    batch, seq, _ = x_prompt.shape
    dec_batch, dec_seq, _ = x_sample.shape
    assert dec_seq == 1 and seq % MOBA_BLOCK == 0
    n_pages = page_table.shape[1]
    past_len = n_pages * PAGE_SIZE
    assert past_len % MOBA_BLOCK == 0 and past_len // MOBA_BLOCK >= MOBA_TOPK
    n_phys = cache_moba_k.shape[1]

    wmo = w_moba_o[0].astype(BF16)
    wdo = w_diff_o[0].astype(BF16)
    wo = w_out[0].astype(BF16)
    lams = (lambda_q1, lambda_k1, lambda_q2, lambda_k2)
    lng = ln_g.reshape(1, D_MODEL)
    lnb = ln_b.reshape(1, D_MODEL)

    xp = x_prompt.reshape(batch * seq, D_MODEL)
    qm, km, vm, zm, qd, kd, vd, zd, gt, kbm, vtm, kmm, kbd, vtd = _proj(
        xp.astype(BF16), w_in[0], tm=PROJ_TM, attn_layouts=True)
    xs = x_sample.reshape(dec_batch, D_MODEL)
    qm_s, km_s, vm_s, zm_s, qd_s, kd_s, vd_s, zd_s, gt_s = _proj(
        xs.astype(BF16), w_in[0], tm=dec_batch, attn_layouts=False)
    cmk3 = cache_moba_k.reshape(n_phys, ROWS_PG, HEAD_DIM)
    cmv3 = cache_moba_v.reshape(n_phys, ROWS_PG, HEAD_DIM)

    def diff_pages(c):
        c = c.reshape(n_phys, PAGE_SIZE, DIFF_KV_HEADS, 2, HEAD_DIM)
        return jnp.transpose(c, (0, 1, 3, 2, 4)).reshape(n_phys, ROWS_PG, HEAD_DIM)

    cdk3 = diff_pages(cache_diff_k)
    cdv3 = diff_pages(cache_diff_v)

    def pair_rows(a):
        a = jnp.transpose(a.reshape(dec_batch, 2, DIFF_KV_HEADS, HEAD_DIM), (0, 2, 1, 3))
        a = a.reshape(dec_batch, DIFF_KV_HEADS, 1, 2, HEAD_DIM)
        return jnp.broadcast_to(a, (dec_batch, DIFF_KV_HEADS, 2, 2, HEAD_DIM)).reshape(dec_batch, 8, HEAD_DIM)

    def diff_out(a, *lead):
        a = a.reshape(lead + (2, DIFF_KV_HEADS, HEAD_DIM))
        a = jnp.swapaxes(a, -3, -2)
        return a.reshape((DEPTH,) + lead + (DIFF_KV_HEADS, DIFF_V_DIM))

    qd8 = qd_s.reshape(dec_batch, 8, HEAD_DIM)
    qm8 = qm_s.reshape(dec_batch, 8, HEAD_DIM)
    zd8 = zd_s.reshape(dec_batch, 8, HEAD_DIM)
    zm8 = zm_s.reshape(dec_batch, 8, HEAD_DIM)
    gn8 = diff_norm_g.reshape(8, HEAD_DIM)
    qz = tuple(a.reshape(batch, seq, 1024) for a in (qm, zm, qd, zd))
    prompt_ops = (kbm, vtm, kmm, kbd, vtd, diff_norm_g.reshape(DIFF_KV_HEADS, 2, DIFF_V_DIM)) + qz + qz
    ad_s, sel, am0, ad0, am1, ad1 = _dec1_attn(page_table, qd8, qm8, pair_rows(kd_s), pair_rows(vd_s), zd8, gn8, lams,
                                               cdk3, cdv3, cmk3, prompt_ops, past_len, batch, seq)
    y_p = _post((am0, am1, ad0, ad1), gt, xp, wmo, wdo, wo, lng, lnb, tm=MOBA_BLOCK)
    sel2 = sel[:, :, :MOBA_TOPK].reshape(dec_batch, MOBA_HEADS * MOBA_TOPK)
    kmn8 = jnp.repeat(km_s.reshape(dec_batch, MOBA_KV_HEADS, HEAD_DIM), 2, axis=1)
    vmn8 = jnp.repeat(vm_s.reshape(dec_batch, MOBA_KV_HEADS, HEAD_DIM), 2, axis=1)
    am_s = _dec2(page_table, sel2, qm8, kmn8, vmn8, zm8, cmk3, cmv3, past_len)
    y_s = _post((am_s.reshape(dec_batch, 1024), ad_s.reshape(dec_batch, 1024)), gt_s, xs,
                wmo, wdo, wo, lng, lnb, tm=dec_batch)

    return (y_p.reshape(batch, seq, D_MODEL), y_s.reshape(dec_batch, 1, D_MODEL),
            km.reshape(DEPTH, batch, seq, MOBA_KV_HEADS, HEAD_DIM),
            vm.reshape(DEPTH, batch, seq, MOBA_KV_HEADS, HEAD_DIM),
            diff_out(kd, batch, seq), diff_out(vd, batch, seq),
            km_s.reshape(DEPTH, dec_batch, 1, MOBA_KV_HEADS, HEAD_DIM),
            vm_s.reshape(DEPTH, dec_batch, 1, MOBA_KV_HEADS, HEAD_DIM),
            diff_out(kd_s, dec_batch, 1), diff_out(vd_s, dec_batch, 1))
```
